```python
import math
import jax, jax.numpy as jnp
from jax import lax
import numpy as np

D_MODEL = 2048
BATCH = 2
SEQ = 4096
DEPTH = 4
DEC_BATCH = 128
DEC_SEQ = 4
PAST_LEN = 8192
PAGE_SIZE = 128

N_EVEN = (DEPTH + 1) // 2
N_ODD = DEPTH // 2
MIX_WIDTH = D_MODEL
EPS = 1e-6
NEG_INF = -1e30
Q_BLOCK = 128

POOL_WINDOWS = (2, 4, 8, 16)
POOL_WIDTH = MIX_WIDTH // 4
POOL_GROUP = POOL_WIDTH // len(POOL_WINDOWS)
POOL_HIST = max(POOL_WINDOWS) - 1

MLA_NOPE = 128
MLA_ROPE = 64
MLA_QK = MLA_NOPE + MLA_ROPE
MLA_V = 128
MLA_HEADS = (MIX_WIDTH - POOL_WIDTH) // MLA_V
MLA_Q_RANK = 512
MLA_KV_RANK = 512
MLA_LAT = MLA_KV_RANK + MLA_ROPE
ROPE_THETA = 10000.0

DIFF_DIM = 64
DIFF_HEADS = MIX_WIDTH // (2 * DIFF_DIM)
DIFF_KV_HEADS = 2
DIFF_GROUP = DIFF_HEADS // DIFF_KV_HEADS

EVEN_SPLITS = (POOL_WIDTH, POOL_WIDTH + MLA_Q_RANK, POOL_WIDTH + MLA_Q_RANK + MLA_LAT)
EVEN_IN = EVEN_SPLITS[-1] + MIX_WIDTH
ODD_Q = DIFF_HEADS * 2 * DIFF_DIM
ODD_KV = DIFF_KV_HEADS * 2 * DIFF_DIM
ODD_SPLITS = (ODD_Q, ODD_Q + ODD_KV, ODD_Q + 2 * ODD_KV)
ODD_IN = ODD_SPLITS[-1] + MIX_WIDTH

kernel_name = 'hybrid_pool_mla_diffattn_decoder_step'


def rms_norm(x, g):
    xf = x.astype(jnp.float32)
    xn = xf * lax.rsqrt(jnp.mean(xf * xf, axis=-1, keepdims=True) + EPS)
    return xn.astype(x.dtype) * g


def rope_angles(pos):
    inv = 1.0 / (ROPE_THETA ** (jnp.arange(0, MLA_ROPE, 2, dtype=jnp.float32) / MLA_ROPE))
    ang = pos.astype(jnp.float32)[:, None] * inv[None, :]
    return jnp.cos(ang), jnp.sin(ang)


def apply_rope(x, cos, sin):
    xf = x.astype(jnp.float32)
    half = MLA_ROPE // 2
    x1, x2 = xf[..., :half], xf[..., half:]
    return jnp.concatenate([x1 * cos - x2 * sin, x2 * cos + x1 * sin], axis=-1).astype(x.dtype)


def ada_pre(x, c, g, w_ada, b_ada):
    mod = jax.nn.silu(c) @ w_ada + b_ada
    shift, scale, gate = jnp.split(mod, 3, axis=-1)
    h = rms_norm(x, g) * (1.0 + scale[:, None, :]) + shift[:, None, :]
    return h, gate[:, None, :]


def pool_mix(u_ext, pos, w_mix, scale):
    b, n, _ = u_ext.shape
    t = n - POOL_HIST
    uf = u_ext.astype(jnp.float32)
    csum = jnp.concatenate([jnp.zeros((b, 1, POOL_WIDTH), jnp.float32), jnp.cumsum(uf, axis=1)], axis=1)
    end = csum[:, POOL_HIST + 1:]
    cur = uf[:, POOL_HIST:]
    groups = []
    for gi, w in enumerate(POOL_WINDOWS):
        sl = slice(gi * POOL_GROUP, (gi + 1) * POOL_GROUP)
        start = csum[:, POOL_HIST + 1 - w: POOL_HIST + 1 - w + t, sl]
        cnt = jnp.minimum(w, pos + 1).astype(jnp.float32)[None, :, None]
        groups.append((end[..., sl] - start) / cnt - cur[..., sl])
    a = jnp.stack(groups, axis=2).astype(u_ext.dtype)
    a = jnp.einsum('btgc,gcd->btgd', a, w_mix)
    return a.reshape(b, t, POOL_WIDTH) * scale


def mla_query(q_down, pos, g_qa, w_uq, g_q):
    q = jnp.einsum('btr,rhd->bthd', rms_norm(q_down, g_qa), w_uq)
    q = rms_norm(q, g_q)
    cos, sin = rope_angles(pos)
    q_pe = apply_rope(q[..., MLA_NOPE:], cos[:, None, :], sin[:, None, :])
    return jnp.concatenate([q[..., :MLA_NOPE], q_pe], axis=-1)


def mla_latent(kv_down, pos, g_kv):
    cos, sin = rope_angles(pos)
    c_kv = rms_norm(kv_down[..., :MLA_KV_RANK], g_kv)
    k_pe = apply_rope(kv_down[..., MLA_KV_RANK:], cos, sin)
    return jnp.concatenate([c_kv, k_pe], axis=-1)


def mla_keys(lat, w_uk, g_k):
    k_nope = jnp.einsum('...sr,rhd->...shd', lat[..., :MLA_KV_RANK], w_uk)
    k_pe = jnp.broadcast_to(lat[..., None, MLA_KV_RANK:], k_nope.shape[:-1] + (MLA_ROPE,))
    gain = jnp.concatenate([g_k, jnp.ones((MLA_ROPE,), g_k.dtype)])
    return rms_norm(jnp.concatenate([k_nope, k_pe], axis=-1), gain)


def causal_softmax(s, qpos, kpos):
    s = jnp.where(kpos[None, :] <= qpos[:, None], s, NEG_INF)
    return jax.nn.softmax(s, axis=-1)


def mla_probs(q, k, qpos, kpos):
    s = jnp.einsum('...qhd,...shd->...hqs', q, k).astype(jnp.float32) * (MLA_QK ** -0.5)
    return causal_softmax(s, qpos, kpos)


def diff_attend(q, k, v, lam, qpos, kpos):
    k = k.reshape(k.shape[:-1] + (2, DIFF_DIM))
    s = jnp.einsum('...qkgmd,...skmd->...kgmqs', q, k).astype(jnp.float32) * (DIFF_DIM ** -0.5)
    p = causal_softmax(s, qpos, kpos)
    w = p[..., 0, :, :] - lam * p[..., 1, :, :]
    o = jnp.einsum('...kgqs,...skv->...qkgv', w.astype(v.dtype), v)
    return o.reshape(o.shape[:-3] + (DIFF_HEADS, 2 * DIFF_DIM))


def sweep_query_blocks(q, fn):
    b, s = q.shape[:2]
    nb = s // Q_BLOCK
    qb = jnp.moveaxis(q.reshape((b, nb, Q_BLOCK) + q.shape[2:]), 1, 0)

    def body(args):
        qi, bi = args
        return fn(qi, bi * Q_BLOCK + jnp.arange(Q_BLOCK))

    o = jnp.moveaxis(lax.map(body, (qb, jnp.arange(nb))), 0, 1)
    return o.reshape((b, s) + o.shape[3:])


def setup_inputs(seed: int = 0) -> dict:
    key = jax.random.key(seed)
    ks = list(jax.random.split(key, 40))
    f32 = jnp.float32
    n_pages = PAST_LEN // PAGE_SIZE
    n_pool = (DEC_BATCH * n_pages * 5) // 4

    def nrm(shape, scale=1.0):
        return jax.random.normal(ks.pop(), shape, f32) * scale

    def gain(shape, s=0.02):
        return 1.0 + nrm(shape, s)

    perm = jax.random.permutation(ks.pop(), n_pool)
    page_table = perm[:DEC_BATCH * n_pages].reshape(DEC_BATCH, n_pages).astype(jnp.int32)
    return {
        'x_prompt': nrm((BATCH, SEQ, D_MODEL)),
        'x_sample': nrm((DEC_BATCH, DEC_SEQ, D_MODEL)),
        'cache_mla': nrm((N_EVEN, n_pool, PAGE_SIZE, MLA_LAT)),
        'cache_diff_k': nrm((N_ODD, n_pool, PAGE_SIZE, DIFF_KV_HEADS, 2 * DIFF_DIM)),
        'cache_diff_v': nrm((N_ODD, n_pool, PAGE_SIZE, DIFF_KV_HEADS, 2 * DIFF_DIM)),
        'state_pool': nrm((N_EVEN, DEC_BATCH, POOL_HIST, POOL_WIDTH)),
        'page_table': page_table,
        'c_prompt': nrm((BATCH, D_MODEL)),
        'c_sample': nrm((DEC_BATCH, D_MODEL)),
        'norm_g': gain((DEPTH, D_MODEL)),
        'w_ada': nrm((DEPTH, D_MODEL, 3 * D_MODEL), 0.5 * D_MODEL ** -0.5),
        'b_ada': nrm((DEPTH, 3 * D_MODEL), 0.02),
        'w_in_e': nrm((N_EVEN, D_MODEL, EVEN_IN), D_MODEL ** -0.5),
        'w_pool_mix': nrm((N_EVEN, len(POOL_WINDOWS), POOL_GROUP, POOL_GROUP), POOL_GROUP ** -0.5),
        'pool_scale': gain((N_EVEN, POOL_WIDTH), 0.1),
        'g_qa': gain((N_EVEN, MLA_Q_RANK)),
        'g_kv': gain((N_EVEN, MLA_KV_RANK)),
        'w_uq': nrm((N_EVEN, MLA_Q_RANK, MLA_HEADS, MLA_QK), MLA_Q_RANK ** -0.5),
        'w_uk': nrm((N_EVEN, MLA_KV_RANK, MLA_HEADS, MLA_NOPE), MLA_KV_RANK ** -0.5),
        'w_uv': nrm((N_EVEN, MLA_KV_RANK, MLA_HEADS, MLA_V), MLA_KV_RANK ** -0.5),
        'g_q': gain((N_EVEN, MLA_QK)),
        'g_k': gain((N_EVEN, MLA_NOPE)),
        'w_out_e': nrm((N_EVEN, MIX_WIDTH, D_MODEL), MIX_WIDTH ** -0.5),
        'w_in_o': nrm((N_ODD, D_MODEL, ODD_IN), D_MODEL ** -0.5),
        'g_dq': gain((N_ODD, 2, DIFF_DIM)),
        'g_dk': gain((N_ODD, 2, DIFF_DIM)),
        'lam_q1': nrm((N_ODD, DIFF_DIM), 0.1),
        'lam_k1': nrm((N_ODD, DIFF_DIM), 0.1),
        'lam_q2': nrm((N_ODD, DIFF_DIM), 0.1),
        'lam_k2': nrm((N_ODD, DIFF_DIM), 0.1),
        'g_subln': gain((N_ODD, 2 * DIFF_DIM)),
        'w_out_o': nrm((N_ODD, MIX_WIDTH, D_MODEL), MIX_WIDTH ** -0.5),
    }


def reference(x_prompt, x_sample, cache_mla, cache_diff_k, cache_diff_v, state_pool, page_table,
              c_prompt, c_sample, norm_g, w_ada, b_ada, w_in_e, w_pool_mix, pool_scale, g_qa, g_kv,
              w_uq, w_uk, w_uv, g_q, g_k, w_out_e, w_in_o, g_dq, g_dk, lam_q1, lam_k1, lam_q2, lam_k2,
              g_subln, w_out_o):
    f32 = jnp.float32
    past_len = page_table.shape[1] * cache_mla.shape[2]
    s_len = x_prompt.shape[1]
    t_len = x_sample.shape[1]
    pos_p = jnp.arange(s_len)
    pos_s = past_len + jnp.arange(t_len)
    kpos_s = jnp.arange(past_len + t_len)

    def mla_prompt(j, q, lat):
        k = mla_keys(lat, w_uk[j], g_k[j])
        v = jnp.einsum('bsr,rhd->bshd', lat[..., :MLA_KV_RANK], w_uv[j])

        def blk(qi, qpos):
            p = mla_probs(qi, k, qpos, pos_p)
            return jnp.einsum('bhqs,bshd->bqhd', p.astype(v.dtype), v)

        o = sweep_query_blocks(q, blk)
        return o.reshape(o.shape[:2] + (MLA_HEADS * MLA_V,))

    def mla_sample(j, q, lat_new):
        def one(args):
            qi, ln, pt = args
            lat = jnp.concatenate([cache_mla[j, pt].reshape(past_len, MLA_LAT), ln], axis=0)
            p = mla_probs(qi, mla_keys(lat, w_uk[j], g_k[j]), pos_s, kpos_s)
            o_lat = jnp.einsum('hqs,sr->qhr', p.astype(lat.dtype), lat[:, :MLA_KV_RANK])
            return jnp.einsum('qhr,rhd->qhd', o_lat, w_uv[j]).reshape(t_len, MLA_HEADS * MLA_V)

        return lax.map(one, (q, lat_new, page_table))

    def diff_prompt(j, q, k, v, lam):
        return sweep_query_blocks(q, lambda qi, qpos: diff_attend(qi, k, v, lam, qpos, pos_p))

    def diff_sample(j, q, k_new, v_new, lam):
        def one(args):
            qi, kn, vn, pt = args
            k = jnp.concatenate([cache_diff_k[j, pt].reshape((past_len,) + kn.shape[1:]), kn], axis=0)
            v = jnp.concatenate([cache_diff_v[j, pt].reshape((past_len,) + vn.shape[1:]), vn], axis=0)
            return diff_attend(qi, k, v, lam, pos_s, kpos_s)

        return lax.map(one, (q, k_new, v_new, page_table))

    def run(x, c, pos, pool_hist_fn, mla_fn, diff_fn):
        lats, tails, ks, vs = [], [], [], []
        b, t = x.shape[:2]
        for i in range(DEPTH):
            j = i // 2
            h, gate_res = ada_pre(x, c, norm_g[i], w_ada[i], b_ada[i])
            if i % 2 == 0:
                u, q_down, kv_down, gate = jnp.split(h @ w_in_e[j], list(EVEN_SPLITS), axis=-1)
                u_ext = jnp.concatenate([pool_hist_fn(j), u], axis=1)
                a_out = pool_mix(u_ext, pos, w_pool_mix[j], pool_scale[j])
                lat = mla_latent(kv_down, pos, g_kv[j])
                b_out = mla_fn(j, mla_query(q_down, pos, g_qa[j], w_uq[j], g_q[j]), lat)
                y = (jnp.concatenate([a_out, b_out], axis=-1) * jax.nn.silu(gate)) @ w_out_e[j]
                lats.append(lat)
                tails.append(u_ext[:, -POOL_HIST:])
            else:
                qd, kd, vd, gate = jnp.split(h @ w_in_o[j], list(ODD_SPLITS), axis=-1)
                q = rms_norm(qd.reshape(b, t, DIFF_KV_HEADS, DIFF_GROUP, 2, DIFF_DIM), g_dq[j])
                k = rms_norm(kd.reshape(b, t, DIFF_KV_HEADS, 2, DIFF_DIM), g_dk[j])
                k = k.reshape(b, t, DIFF_KV_HEADS, 2 * DIFF_DIM)
                v = vd.reshape(b, t, DIFF_KV_HEADS, 2 * DIFF_DIM)
                lam_init = 0.8 - 0.6 * math.exp(-0.3 * i)
                lam = (jnp.exp(jnp.sum(lam_q1[j].astype(f32) * lam_k1[j].astype(f32)))
                       - jnp.exp(jnp.sum(lam_q2[j].astype(f32) * lam_k2[j].astype(f32))) + lam_init)
                o = rms_norm(diff_fn(j, q, k, v, lam), g_subln[j]) * (1.0 - lam_init)
                y = (o.reshape(b, t, MIX_WIDTH) * jax.nn.silu(gate)) @ w_out_o[j]
                ks.append(k)
                vs.append(v)
            x = x + gate_res * y
        return x, jnp.stack(lats), jnp.stack(tails), jnp.stack(ks), jnp.stack(vs)

    zero_hist = jnp.zeros((x_prompt.shape[0], POOL_HIST, POOL_WIDTH), x_prompt.dtype)
    y_prompt, lat_p, tail_p, k_p, v_p = run(x_prompt, c_prompt, pos_p, lambda j: zero_hist,
                                            mla_prompt, diff_prompt)
    y_sample, lat_s, tail_s, k_s, v_s = run(x_sample, c_sample, pos_s, lambda j: state_pool[j],
                                            mla_sample, diff_sample)
    return (y_prompt, y_sample, lat_p, lat_s, k_p, k_s, v_p, v_s, tail_p, tail_s)
```

```python
import functools
import math

import jax
import jax.numpy as jnp
from jax import lax
from jax.experimental import pallas as pl
from jax.experimental.pallas import tpu as pltpu

F32 = jnp.float32
BF16 = jnp.bfloat16

EPS = 1e-6
NEG_INF = -1e30
ROPE_THETA = 10000.0

POOL_WINDOWS = (2, 4, 8, 16)
POOL_GROUP = 128
POOL_WIDTH = 512
POOL_HIST = 15
HIST_ROWS = 16

MLA_NOPE = 128
MLA_ROPE = 64
MLA_QK = 192
MLA_V = 128
MLA_HEADS = 12
MLA_RANK = 512
MLA_LAT = 576
HEAD_BLOCK = 256
Q_ROWS = 8

DIFF_DIM = 64
DIFF_HEADS = 16
DIFF_KV_HEADS = 2
DIFF_GROUP = 8
DIFF_V = 128

NEW_PAD = 128

VMEM_LIMIT = 56 * 1024 * 1024


def _cparams(*sem):
    return pltpu.CompilerParams(dimension_semantics=sem, vmem_limit_bytes=VMEM_LIMIT)


def _dot(a, b):
    return jnp.dot(a, b, preferred_element_type=F32)


def _dot_nt(a, b):
    return lax.dot_general(a, b, (((1,), (1,)), ((), ())), preferred_element_type=F32)


def _silu(x):
    return x * jax.nn.sigmoid(x)


def _lane_lo(shape):
    return (lax.broadcasted_iota(jnp.int32, shape, len(shape) - 1) % 128) < 64


def _ada_kernel(c_ref, w_ref, b_ref, o_ref):
    s = _silu(c_ref[...]).astype(BF16)
    o_ref[...] = _dot(s, w_ref[...].astype(BF16)) + b_ref[...]


def ada_mods(c_all, w_ada, b_ada):
    depth, d, n3 = w_ada.shape
    mc = c_all.shape[0]
    tn = 512
    return pl.pallas_call(
        _ada_kernel,
        grid=(depth, n3 // tn),
        in_specs=[pl.BlockSpec((mc, d), lambda i, j: (0, 0)),
                  pl.BlockSpec((None, d, tn), lambda i, j: (i, 0, j)),
                  pl.BlockSpec((None, 1, tn), lambda i, j: (i, 0, j))],
        out_specs=pl.BlockSpec((None, mc, tn), lambda i, j: (i, 0, j)),
        out_shape=jax.ShapeDtypeStruct((depth, mc, n3), F32),
        compiler_params=_cparams("parallel", "parallel"),
        name="ada_mods",
    )(c_all, w_ada, b_ada.reshape(depth, 1, n3))


def _modnorm_proj_kernel(x_ref, sc_ref, sh_ref, g_ref, w_ref, o_ref, h_scr):
    @pl.when(pl.program_id(2) == 0)
    def _():
        x = x_ref[...]
        xn = x * lax.rsqrt(jnp.mean(x * x, axis=-1, keepdims=True) + EPS)
        h = xn * g_ref[...] * (1.0 + sc_ref[...]) + sh_ref[...]
        h_scr[...] = h.astype(BF16)

    o_ref[...] = _dot(h_scr[...], w_ref[...])


def _mod_spec(mod, tm):
    d = mod.shape[-1]
    if mod.shape[1] == 1:
        return pl.BlockSpec((None, 1, d), lambda b, i, j: (b, 0, 0))
    return pl.BlockSpec((None, tm, d), lambda b, i, j: (b, i, 0))


def modnorm_proj(x3, scale3, shift3, g, w, tn):
    gsz, r, d = x3.shape
    n = w.shape[1]
    tm = min(512, r)
    return pl.pallas_call(
        _modnorm_proj_kernel,
        grid=(gsz, r // tm, n // tn),
        in_specs=[pl.BlockSpec((None, tm, d), lambda b, i, j: (b, i, 0)),
                  _mod_spec(scale3, tm), _mod_spec(shift3, tm),
                  pl.BlockSpec((1, d), lambda b, i, j: (0, 0)),
                  pl.BlockSpec((d, tn), lambda b, i, j: (0, j))],
        out_specs=pl.BlockSpec((None, tm, tn), lambda b, i, j: (b, i, j)),
        out_shape=jax.ShapeDtypeStruct((gsz, r, n), F32),
        scratch_shapes=[pltpu.VMEM((tm, d), BF16)],
        compiler_params=_cparams("parallel", "parallel", "arbitrary"),
        name="modnorm_proj",
    )(x3, scale3, shift3, g.reshape(1, d), w)


def _latent(kvc, pe_blk, tab, gkv):
    c = kvc * lax.rsqrt(jnp.mean(kvc * kvc, axis=-1, keepdims=True) + EPS) * gkv
    u = pe_blk * tab
    kpe = (u + pltpu.roll(u, 64, 1))[:, :MLA_ROPE]
    return c, kpe


def _query_heads(qd, gqa, wuq, gq, tab, scale):
    qn = (qd * lax.rsqrt(jnp.mean(qd * qd, axis=-1, keepdims=True) + EPS) * gqa).astype(BF16)
    q_all = _dot(qn, wuq)
    g_nope = gq[:, :MLA_NOPE]
    rope_mul = tab * gq[:, MLA_NOPE:]
    lo = _lane_lo((qd.shape[0], 128))
    heads = []
    for h in range(MLA_HEADS):
        nope = q_all[:, h * HEAD_BLOCK:h * HEAD_BLOCK + MLA_NOPE]
        t = q_all[:, h * HEAD_BLOCK + MLA_NOPE:(h + 1) * HEAD_BLOCK]
        ss = (jnp.sum(nope * nope, axis=-1, keepdims=True)
              + jnp.sum(jnp.where(lo, t * t, 0.0), axis=-1, keepdims=True))
        r = lax.rsqrt(ss * (1.0 / MLA_QK) + EPS) * scale
        u = t * rope_mul
        pe = (u + pltpu.roll(u, 64, 1))[:, :MLA_ROPE] * r
        heads.append((nope * g_nope * r, pe))
    return heads


def _even_prep_prompt_kernel(u_ref, up_ref, qd_ref, kvc_ref, pe_ref, tab_ref, wmix_ref, psc_ref,
                             gqa_ref, gkv_ref, wuq_ref, gq_ref, wuk_ref, gk_ref, wuv_ref,
                             a_ref, lat_ref, q_ref, k_ref, v_ref, uext_scr, *, tm):
    i = pl.program_id(1)
    tab = tab_ref[...]

    u = u_ref[...]
    hist = up_ref[...]
    uext_scr[0:HIST_ROWS, :] = jnp.where(i > 0, hist, jnp.zeros_like(hist))
    uext_scr[HIST_ROWS:, :] = u
    pos = i * tm + lax.broadcasted_iota(jnp.int32, (tm, 1), 0)
    for gi, w in enumerate(POOL_WINDOWS):
        sl = slice(gi * POOL_GROUP, (gi + 1) * POOL_GROUP)
        acc = u[:, sl]
        for k in range(1, w):
            acc = acc + uext_scr[HIST_ROWS - k:HIST_ROWS - k + tm, sl]
        cnt = jnp.minimum(w, pos + 1).astype(F32)
        a = (acc / cnt - u[:, sl]).astype(BF16)
        a_ref[:, sl] = _dot(a, wmix_ref[gi]) * psc_ref[:, sl]

    c, kpe = _latent(kvc_ref[...], pe_ref[...], tab, gkv_ref[...])
    lat_ref[:, :MLA_RANK] = c
    lat_ref[:, MLA_RANK:] = kpe
    heads = _query_heads(qd_ref[...], gqa_ref[...], wuq_ref[...], gq_ref[...], tab, MLA_QK ** -0.5)
    cb = c.astype(BF16)
    k_all = _dot(cb, wuk_ref[...])
    v_all = _dot(cb, wuv_ref[...])
    pe_ss = jnp.sum(kpe * kpe, axis=-1, keepdims=True)
    gk = gk_ref[...]
    for h in range(MLA_HEADS):
        nope, pe = heads[h]
        q_ref[h, :, :MLA_NOPE] = nope.astype(BF16)
        q_ref[h, :, MLA_NOPE:] = pe.astype(BF16)
        kn = k_all[:, h * MLA_NOPE:(h + 1) * MLA_NOPE]
        r = lax.rsqrt((jnp.sum(kn * kn, axis=-1, keepdims=True) + pe_ss) * (1.0 / MLA_QK) + EPS)
        k_ref[h, :, :MLA_NOPE] = (kn * gk * r).astype(BF16)
        k_ref[h, :, MLA_NOPE:] = (kpe * r).astype(BF16)
        v_ref[h] = v_all[:, h * MLA_V:(h + 1) * MLA_V].astype(BF16)


def even_prep_prompt(proj, tab, wmix, psc, gqa, gkv, wuq, gq, wuk, gk, wuv):
    b, s, _ = proj.shape
    tm = min(256, s)
    hb = tm // HIST_ROWS
    full = lambda a: pl.BlockSpec(a.shape, lambda bb, i: (0,) * a.ndim)
    head_out = lambda w: pl.BlockSpec((None, MLA_HEADS, tm, w), lambda bb, i: (bb, 0, i, 0))
    return pl.pallas_call(
        functools.partial(_even_prep_prompt_kernel, tm=tm),
        grid=(b, s // tm),
        in_specs=[pl.BlockSpec((None, tm, 512), lambda bb, i: (bb, i, 4)),
                  pl.BlockSpec((None, HIST_ROWS, 512), lambda bb, i: (bb, jnp.maximum(i * hb - 1, 0), 4)),
                  pl.BlockSpec((None, tm, 512), lambda bb, i: (bb, i, 5)),
                  pl.BlockSpec((None, tm, 512), lambda bb, i: (bb, i, 6)),
                  pl.BlockSpec((None, tm, 128), lambda bb, i: (bb, i, 28)),
                  pl.BlockSpec((tm, 128), lambda bb, i: (i, 0)),
                  full(wmix), full(psc), full(gqa), full(gkv), full(wuq), full(gq), full(wuk), full(gk),
                  full(wuv)],
        out_specs=[pl.BlockSpec((None, tm, POOL_WIDTH), lambda bb, i: (bb, i, 0)),
                   pl.BlockSpec((None, tm, MLA_LAT), lambda bb, i: (bb, i, 0)),
                   head_out(MLA_QK), head_out(MLA_QK), head_out(MLA_V)],
        out_shape=[jax.ShapeDtypeStruct((b, s, POOL_WIDTH), F32),
                   jax.ShapeDtypeStruct((b, s, MLA_LAT), F32),
                   jax.ShapeDtypeStruct((b, MLA_HEADS, s, MLA_QK), BF16),
                   jax.ShapeDtypeStruct((b, MLA_HEADS, s, MLA_QK), BF16),
                   jax.ShapeDtypeStruct((b, MLA_HEADS, s, MLA_V), BF16)],
        scratch_shapes=[pltpu.VMEM((HIST_ROWS + tm, POOL_WIDTH), F32)],
        compiler_params=_cparams("parallel", "parallel"),
        name="even_prep_prompt",
    )(proj, proj, proj, proj, proj, tab, wmix, psc, gqa, gkv, wuq, gq, wuk, gk, wuv)


def _even_prep_sample_kernel(qd_ref, kvc_ref, pe_ref, tab_ref, gqa_ref, gkv_ref, wuq_ref, gq_ref,
                             gk_ref, wukt_ref, lat_ref, qabs_ref, qpe_ref):
    tab = tab_ref[...]
    c, kpe = _latent(kvc_ref[...], pe_ref[...], tab, gkv_ref[...])
    lat_ref[:, :MLA_RANK] = c
    lat_ref[:, MLA_RANK:] = kpe
    heads = _query_heads(qd_ref[...], gqa_ref[...], wuq_ref[...], gq_ref[...], tab, MLA_QK ** -0.5)
    gk = gk_ref[...]
    for h in range(MLA_HEADS):
        nope, pe = heads[h]
        qabs_ref[h] = _dot((nope * gk).astype(BF16), wukt_ref[h]).astype(BF16)
        qpe_ref[h] = pe.astype(BF16)


def even_prep_sample(proj, tab, gqa, gkv, wuq, gq, gk, wukt):
    _, t, _ = proj.shape
    tm = min(256, t)
    full = lambda a: pl.BlockSpec(a.shape, lambda i: (0,) * a.ndim)
    return pl.pallas_call(
        _even_prep_sample_kernel,
        grid=(t // tm,),
        in_specs=[pl.BlockSpec((None, tm, 512), lambda i: (0, i, 5)),
                  pl.BlockSpec((None, tm, 512), lambda i: (0, i, 6)),
                  pl.BlockSpec((None, tm, 128), lambda i: (0, i, 28)),
                  pl.BlockSpec((tm, 128), lambda i: (i, 0)),
                  full(gqa), full(gkv), full(wuq), full(gq), full(gk), full(wukt)],
        out_specs=[pl.BlockSpec((tm, MLA_LAT), lambda i: (i, 0)),
                   pl.BlockSpec((MLA_HEADS, tm, MLA_RANK), lambda i: (0, i, 0)),
                   pl.BlockSpec((MLA_HEADS, tm, MLA_ROPE), lambda i: (0, i, 0))],
        out_shape=[jax.ShapeDtypeStruct((t, MLA_LAT), F32),
                   jax.ShapeDtypeStruct((MLA_HEADS, t, MLA_RANK), BF16),
                   jax.ShapeDtypeStruct((MLA_HEADS, t, MLA_ROPE), BF16)],
        compiler_params=_cparams("parallel"),
        name="even_prep_sample",
    )(proj, proj, proj, tab, gqa, gkv, wuq, gq, gk, wukt)


def _pool_sample_kernel(st_ref, u_ref, wmix_ref, psc_ref, a_ref, *, t_len):
    rows = [st_ref[:, r, :] for r in range(POOL_HIST)] + [u_ref[:, r, :] for r in range(t_len)]
    for t in range(t_len):
        cur = rows[POOL_HIST + t]
        for gi, w in enumerate(POOL_WINDOWS):
            sl = slice(gi * POOL_GROUP, (gi + 1) * POOL_GROUP)
            acc = cur[:, sl]
            for k in range(1, w):
                acc = acc + rows[POOL_HIST + t - k][:, sl]
            a = (acc / float(w) - cur[:, sl]).astype(BF16)
            a_ref[:, t, sl] = _dot(a, wmix_ref[gi]) * psc_ref[:, sl]


def pool_sample(state, u, wmix, psc):
    b, t_len, _ = u.shape
    tb = min(64, b)
    full = lambda a: pl.BlockSpec(a.shape, lambda i: (0,) * a.ndim)
    return pl.pallas_call(
        functools.partial(_pool_sample_kernel, t_len=t_len),
        grid=(b // tb,),
        in_specs=[pl.BlockSpec((tb, POOL_HIST, POOL_WIDTH), lambda i: (i, 0, 0)),
                  pl.BlockSpec((tb, t_len, POOL_WIDTH), lambda i: (i, 0, 0)),
                  full(wmix), full(psc)],
        out_specs=pl.BlockSpec((tb, t_len, POOL_WIDTH), lambda i: (i, 0, 0)),
        out_shape=jax.ShapeDtypeStruct((b, t_len, POOL_WIDTH), F32),
        compiler_params=_cparams("parallel"),
        name="pool_sample",
    )(state, u, wmix, psc)


def _mla_flash_kernel(q_ref, k_ref, v_ref, o_ref, m_scr, l_scr, acc_scr, *, tq):
    i = pl.program_id(2)
    q = q_ref[...]
    m_scr[...] = jnp.full_like(m_scr, NEG_INF)
    l_scr[...] = jnp.zeros_like(l_scr)
    acc_scr[...] = jnp.zeros_like(acc_scr)

    def tile(j, masked):
        off = pl.multiple_of(j * tq, tq)
        s = _dot_nt(q, k_ref[pl.ds(off, tq), :])
        if masked:
            row = lax.broadcasted_iota(jnp.int32, s.shape, 0)
            col = lax.broadcasted_iota(jnp.int32, s.shape, 1)
            s = jnp.where(col <= row, s, NEG_INF)
        m_prev = m_scr[...]
        m_new = jnp.maximum(m_prev, jnp.max(s, axis=-1, keepdims=True))
        alpha = jnp.exp(m_prev - m_new)
        p = jnp.exp(s - m_new)
        l_scr[...] = alpha * l_scr[...] + jnp.sum(p, axis=-1, keepdims=True)
        acc_scr[...] = alpha * acc_scr[...] + _dot(p.astype(BF16), v_ref[pl.ds(off, tq), :])
        m_scr[...] = m_new

    def body(j, carry):
        tile(j, False)
        return carry

    lax.fori_loop(0, i, body, 0)
    tile(i, True)
    o_ref[...] = acc_scr[...] / l_scr[...]


def mla_flash(q, k, v):
    b, h, s, _ = q.shape
    tq = min(512, s)
    return pl.pallas_call(
        functools.partial(_mla_flash_kernel, tq=tq),
        grid=(b, h, s // tq),
        in_specs=[pl.BlockSpec((None, None, tq, MLA_QK), lambda bb, hh, i: (bb, hh, i, 0)),
                  pl.BlockSpec((None, None, s, MLA_QK), lambda bb, hh, i: (bb, hh, 0, 0)),
                  pl.BlockSpec((None, None, s, MLA_V), lambda bb, hh, i: (bb, hh, 0, 0))],
        out_specs=pl.BlockSpec((None, tq, MLA_V), lambda bb, hh, i: (bb, i, hh)),
        out_shape=jax.ShapeDtypeStruct((b, s, h * MLA_V), F32),
        scratch_shapes=[pltpu.VMEM((tq, 1), F32), pltpu.VMEM((tq, 1), F32), pltpu.VMEM((tq, MLA_V), F32)],
        compiler_params=_cparams("parallel", "parallel", "arbitrary"),
        name="mla_flash",
    )(q, k, v)


def _mla_decode_kernel(pt_ref, wt_ref, qabs_ref, qpe_ref, new_ref, *rest, n_pages_step, chunk, t_len):
    pages = rest[:n_pages_step]
    o_ref = rest[n_pages_step]
    lhs_scr, lhspe_scr, latc_scr, pe2_scr, m_scr, l_scr, acc_scr = rest[n_pages_step + 1:]
    b = pl.program_id(0)
    c = pl.program_id(1)
    nk = MLA_HEADS * MLA_NOPE
    nq = MLA_HEADS * Q_ROWS

    @pl.when(jnp.logical_and(b == 0, c == 0))
    def _():
        lhs_scr[0:nk, :] = wt_ref[...]
        lhspe_scr[nq:, :] = jnp.where(_lane_lo((Q_ROWS, 128)), 0.0, 1.0).astype(BF16)

    @pl.when(c == 0)
    def _():
        lhs_scr[nk:, :] = qabs_ref[...]
        lhspe_scr[0:nq, :] = qpe_ref[...]
        m_scr[...] = jnp.full_like(m_scr, NEG_INF)
        l_scr[...] = jnp.zeros_like(l_scr)
        acc_scr[...] = jnp.zeros_like(acc_scr)

    def stage(dst, rows):
        x = rows
        latc_scr[dst:dst + x.shape[0], :] = x[:, :MLA_RANK].astype(BF16)
        pe = x[:, MLA_RANK:]
        pe2_scr[dst:dst + x.shape[0], :] = jnp.concatenate([pe, pe * pe], axis=-1).astype(BF16)

    def attend(off, size, masked):
        latc = latc_scr[pl.ds(off, size), :]
        big = _dot_nt(lhs_scr[...], latc)
        small = _dot_nt(lhspe_scr[...], pe2_scr[pl.ds(off, size), :])
        pe_ss = small[nq:nq + 1, :]
        parts = []
        for h in range(MLA_HEADS):
            kh = big[h * MLA_NOPE:(h + 1) * MLA_NOPE, :]
            ss = jnp.sum(kh * kh, axis=0, keepdims=True) + pe_ss
            r = lax.rsqrt(ss * (1.0 / MLA_QK) + EPS)
            sl = slice(h * Q_ROWS, (h + 1) * Q_ROWS)
            parts.append((big[nk + h * Q_ROWS:nk + (h + 1) * Q_ROWS, :] + small[sl, :]) * r)
        s = jnp.concatenate(parts, axis=0)
        if masked:
            t = lax.broadcasted_iota(jnp.int32, s.shape, 0) % Q_ROWS
            col = lax.broadcasted_iota(jnp.int32, s.shape, 1)
            s = jnp.where(col <= jnp.minimum(t, t_len - 1), s, NEG_INF)
        m_prev = m_scr[...]
        m_new = jnp.maximum(m_prev, jnp.max(s, axis=-1, keepdims=True))
        alpha = jnp.exp(m_prev - m_new)
        p = jnp.exp(s - m_new)
        l_scr[...] = alpha * l_scr[...] + jnp.sum(p, axis=-1, keepdims=True)
        acc_scr[...] = alpha * acc_scr[...] + _dot(p.astype(BF16), latc)
        m_scr[...] = m_new

    for pi in range(n_pages_step):
        stage(pi * 128, pages[pi][...])

    def body(ci, carry):
        attend(pl.multiple_of(ci * chunk, chunk), chunk, False)
        return carry

    lax.fori_loop(0, (n_pages_step * 128) // chunk, body, 0)

    @pl.when(c == pl.num_programs(1) - 1)
    def _():
        stage(0, new_ref[...])
        attend(0, NEW_PAD, True)
        o_ref[...] = acc_scr[...] / l_scr[...]


def mla_decode(cache, layer, page_table, wt, qabs, qpe, lat_new_pad, t_len):
    bsz, n_pages = page_table.shape
    nps = 16 if n_pages % 16 == 0 else (2 if n_pages % 2 == 0 else 1)
    chunk = min(512, nps * 128)
    nk = MLA_HEADS * MLA_NOPE
    nq = MLA_HEADS * Q_ROWS
    page_specs = [pl.BlockSpec((None, None, 128, MLA_LAT),
                               lambda b, c, pt, pi=pi: (layer, pt[b, c * nps + pi], 0, 0))
                  for pi in range(nps)]
    grid_spec = pltpu.PrefetchScalarGridSpec(
        num_scalar_prefetch=1,
        grid=(bsz, n_pages // nps),
        in_specs=[pl.BlockSpec((nk, MLA_RANK), lambda b, c, pt: (0, 0)),
                  pl.BlockSpec((None, nq, MLA_RANK), lambda b, c, pt: (b, 0, 0)),
                  pl.BlockSpec((None, nq, 128), lambda b, c, pt: (b, 0, 0)),
                  pl.BlockSpec((None, NEW_PAD, MLA_LAT), lambda b, c, pt: (b, 0, 0))] + page_specs,
        out_specs=pl.BlockSpec((None, nq, MLA_RANK), lambda b, c, pt: (b, 0, 0)),
        scratch_shapes=[pltpu.VMEM((nk + nq, MLA_RANK), BF16),
                        pltpu.VMEM((nq + 8, 128), BF16),
                        pltpu.VMEM((nps * 128, MLA_RANK), BF16),
                        pltpu.VMEM((nps * 128, 128), BF16),
                        pltpu.VMEM((nq, 1), F32), pltpu.VMEM((nq, 1), F32),
                        pltpu.VMEM((nq, MLA_RANK), F32)])
    return pl.pallas_call(
        functools.partial(_mla_decode_kernel, n_pages_step=nps, chunk=chunk, t_len=t_len),
        grid_spec=grid_spec,
        out_shape=jax.ShapeDtypeStruct((bsz, nq, MLA_RANK), F32),
        compiler_params=_cparams("arbitrary", "arbitrary"),
        name="mla_decode",
    )(page_table, wt, qabs, qpe, lat_new_pad, *([cache] * nps))


def _value_up_kernel(o_ref, w_ref, out_ref):
    out_ref[...] = _dot(o_ref[...].astype(BF16), w_ref[...])


def value_up(o_lat, wuv):
    h, t, r = o_lat.shape
    return pl.pallas_call(
        _value_up_kernel,
        grid=(h,),
        in_specs=[pl.BlockSpec((None, t, r), lambda i: (i, 0, 0)),
                  pl.BlockSpec((None, r, MLA_V), lambda i: (i, 0, 0))],
        out_specs=pl.BlockSpec((t, MLA_V), lambda i: (0, i)),
        out_shape=jax.ShapeDtypeStruct((t, h * MLA_V), F32),
        compiler_params=_cparams("parallel"),
        name="value_up",
    )(o_lat, wuv)


def _seg_norm(x, g128, scale):
    lo = _lane_lo((x.shape[0], 128))
    out = []
    for hb in range(x.shape[1] // 128):
        xh = x[:, hb * 128:(hb + 1) * 128]
        sq = xh * xh
        s0 = jnp.sum(jnp.where(lo, sq, 0.0), axis=-1, keepdims=True)
        s1 = jnp.sum(jnp.where(lo, 0.0, sq), axis=-1, keepdims=True)
        r = jnp.where(lo, lax.rsqrt(s0 * (1.0 / DIFF_DIM) + EPS), lax.rsqrt(s1 * (1.0 / DIFF_DIM) + EPS))
        out.append(xh * r * (g128 * scale))
    return out


def _odd_prep_kernel(q_ref, k_ref, v_ref, gq_ref, gk_ref, qo_ref, ko_ref, kb_ref, vb_ref):
    qs = _seg_norm(q_ref[...], gq_ref[...], DIFF_DIM ** -0.5)
    for hb, blk in enumerate(qs):
        qo_ref[:, hb * 128:(hb + 1) * 128] = blk.astype(BF16)
    ks = _seg_norm(k_ref[...], gk_ref[...], 1.0)
    for hb, blk in enumerate(ks):
        ko_ref[:, hb * 128:(hb + 1) * 128] = blk
        kb_ref[:, hb * 128:(hb + 1) * 128] = blk.astype(BF16)
    vb_ref[...] = v_ref[...].astype(BF16)


def odd_prep(proj, gq128, gk128):
    gsz, r, _ = proj.shape
    tm = min(512, r)
    qw, kw = DIFF_HEADS * 128, DIFF_KV_HEADS * 128
    full = lambda a: pl.BlockSpec(a.shape, lambda b, i: (0,) * a.ndim)
    row = lambda w: pl.BlockSpec((None, tm, w), lambda b, i: (b, i, 0))
    return pl.pallas_call(
        _odd_prep_kernel,
        grid=(gsz, r // tm),
        in_specs=[pl.BlockSpec((None, tm, qw), lambda b, i: (b, i, 1)),
                  pl.BlockSpec((None, tm, kw), lambda b, i: (b, i, 16)),
                  pl.BlockSpec((None, tm, kw), lambda b, i: (b, i, 17)),
                  full(gq128), full(gk128)],
        out_specs=[row(qw), row(kw), row(kw), row(kw)],
        out_shape=[jax.ShapeDtypeStruct((gsz, r, qw), BF16),
                   jax.ShapeDtypeStruct((gsz, r, kw), F32),
                   jax.ShapeDtypeStruct((gsz, r, kw), BF16),
                   jax.ShapeDtypeStruct((gsz, r, kw), BF16)],
        compiler_params=_cparams("parallel", "parallel"),
        name="odd_prep",
    )(proj, proj, proj, gq128, gk128)


def _lambda(lam_ref, lam_init):
    v = lam_ref[...]
    a = jnp.sum(v[0:1] * v[1:2], axis=-1, keepdims=True)
    c = jnp.sum(v[2:3] * v[3:4], axis=-1, keepdims=True)
    return jnp.exp(a) - jnp.exp(c) + lam_init


def _diff_flash_kernel(lam_ref, q_ref, k_ref, v_ref, o_ref, qs_scr, m_scr, l_scr, acc_scr, *, tq, tk, lam_init):
    i = pl.program_id(1)
    rows = 2 * DIFF_GROUP * tq
    lo = _lane_lo((tq, 128))
    for kh in range(DIFF_KV_HEADS):
        for g in range(DIFF_GROUP):
            blk = q_ref[:, (kh * DIFF_GROUP + g) * 128:(kh * DIFF_GROUP + g + 1) * 128]
            zero = jnp.zeros_like(blk)
            qs_scr[kh, g * tq:(g + 1) * tq, :] = jnp.where(lo, blk, zero)
            qs_scr[kh, (DIFF_GROUP + g) * tq:(DIFF_GROUP + g + 1) * tq, :] = jnp.where(lo, zero, blk)
    m_scr[...] = jnp.full_like(m_scr, NEG_INF)
    l_scr[...] = jnp.zeros_like(l_scr)
    acc_scr[...] = jnp.zeros_like(acc_scr)

    def tile(j, masked):
        off = pl.multiple_of(j * tk, tk)
        for kh in range(DIFF_KV_HEADS):
            ksl = slice(kh * 128, (kh + 1) * 128)
            s = _dot_nt(qs_scr[kh], k_ref[pl.ds(off, tk), ksl])
            if masked:
                qpos = i * tq + lax.broadcasted_iota(jnp.int32, s.shape, 0) % tq
                kpos = j * tk + lax.broadcasted_iota(jnp.int32, s.shape, 1)
                s = jnp.where(kpos <= qpos, s, NEG_INF)
            m_prev = m_scr[kh]
            m_new = jnp.maximum(m_prev, jnp.max(s, axis=-1, keepdims=True))
            alpha = jnp.exp(m_prev - m_new)
            p = jnp.exp(s - m_new)
            l_scr[kh] = alpha * l_scr[kh] + jnp.sum(p, axis=-1, keepdims=True)
            acc_scr[kh] = alpha * acc_scr[kh] + _dot(p.astype(BF16), v_ref[pl.ds(off, tk), ksl])
            m_scr[kh] = m_new

    def body(j, carry):
        tile(j, False)
        return carry

    n_full = (i * tq) // tk
    lax.fori_loop(0, n_full, body, 0)
    tile(n_full, True)

    lam = _lambda(lam_ref, lam_init)
    half = DIFF_GROUP * tq
    for kh in range(DIFF_KV_HEADS):
        o_all = acc_scr[kh] / l_scr[kh]
        o = o_all[:half] - lam * o_all[half:]
        for g in range(DIFF_GROUP):
            o_ref[:, (kh * DIFF_GROUP + g) * 128:(kh * DIFF_GROUP + g + 1) * 128] = o[g * tq:(g + 1) * tq]
    del rows


def diff_flash(lam_vecs, q, k, v, lam_init):
    b, s, qw = q.shape
    tq = min(128, s)
    tk = min(256, s)
    rows = 2 * DIFF_GROUP * tq
    return pl.pallas_call(
        functools.partial(_diff_flash_kernel, tq=tq, tk=tk, lam_init=lam_init),
        grid=(b, s // tq),
        in_specs=[pl.BlockSpec(lam_vecs.shape, lambda bb, i: (0, 0)),
                  pl.BlockSpec((None, tq, qw), lambda bb, i: (bb, i, 0)),
                  pl.BlockSpec((None, s, k.shape[-1]), lambda bb, i: (bb, 0, 0)),
                  pl.BlockSpec((None, s, v.shape[-1]), lambda bb, i: (bb, 0, 0))],
        out_specs=pl.BlockSpec((None, tq, qw), lambda bb, i: (bb, i, 0)),
        out_shape=jax.ShapeDtypeStruct((b, s, qw), F32),
        scratch_shapes=[pltpu.VMEM((DIFF_KV_HEADS, rows, 128), BF16),
                        pltpu.VMEM((DIFF_KV_HEADS, rows, 1), F32),
                        pltpu.VMEM((DIFF_KV_HEADS, rows, 1), F32),
                        pltpu.VMEM((DIFF_KV_HEADS, rows, DIFF_V), F32)],
        compiler_params=_cparams("parallel", "arbitrary"),
        name="diff_flash",
    )(lam_vecs, q, k, v)


def _diff_decode_kernel(pt_ref, lam_ref, q_ref, knew_ref, vnew_ref, *rest, n_pages_step, chunk, t_len,
                        lam_init):
    kpages = rest[:n_pages_step]
    vpages = rest[n_pages_step:2 * n_pages_step]
    o_ref = rest[2 * n_pages_step]
    k_scr, v_scr, m_scr, l_scr, acc_scr = rest[2 * n_pages_step + 1:]
    c = pl.program_id(1)
    grp = DIFF_GROUP * t_len

    @pl.when(c == 0)
    def _():
        m_scr[...] = jnp.full_like(m_scr, NEG_INF)
        l_scr[...] = jnp.zeros_like(l_scr)
        acc_scr[...] = jnp.zeros_like(acc_scr)

    def attend(kc, vc, masked):
        s = _dot_nt(q_ref[...], kc)
        if masked:
            t = lax.broadcasted_iota(jnp.int32, s.shape, 0) % t_len
            col = lax.broadcasted_iota(jnp.int32, s.shape, 1)
            s = jnp.where(col <= t, s, NEG_INF)
        m_prev = m_scr[...]
        m_new = jnp.maximum(m_prev, jnp.max(s, axis=-1, keepdims=True))
        alpha = jnp.exp(m_prev - m_new)
        p = jnp.exp(s - m_new)
        l_scr[...] = alpha * l_scr[...] + jnp.sum(p, axis=-1, keepdims=True)
        acc_scr[...] = alpha * acc_scr[...] + _dot(p.astype(BF16), vc)
        m_scr[...] = m_new

    for pi in range(n_pages_step):
        for kh in range(DIFF_KV_HEADS):
            k_scr[pi * 128:(pi + 1) * 128, kh * 128:(kh + 1) * 128] = (
                kpages[pi][pl.ds(kh, 128, stride=DIFF_KV_HEADS), :].astype(BF16))
            v_scr[pi * 128:(pi + 1) * 128, kh * 128:(kh + 1) * 128] = (
                vpages[pi][pl.ds(kh, 128, stride=DIFF_KV_HEADS), :].astype(BF16))

    def body(ci, carry):
        off = pl.multiple_of(ci * chunk, chunk)
        attend(k_scr[pl.ds(off, chunk), :], v_scr[pl.ds(off, chunk), :], False)
        return carry

    lax.fori_loop(0, (n_pages_step * 128) // chunk, body, 0)

    @pl.when(c == pl.num_programs(1) - 1)
    def _():
        attend(knew_ref[...], vnew_ref[...], True)
        o_all = acc_scr[...] / l_scr[...]
        lam = _lambda(lam_ref, lam_init)
        for kh in range(DIFF_KV_HEADS):
            base = kh * 2 * grp
            vsl = slice(kh * DIFF_V, (kh + 1) * DIFF_V)
            o_ref[kh] = o_all[base:base + grp, vsl] - lam * o_all[base + grp:base + 2 * grp, vsl]


def diff_decode(cache_k, cache_v, layer, page_table, lam_vecs, qblk, knew, vnew, t_len, lam_init):
    bsz, n_pages = page_table.shape
    nps = 16 if n_pages % 16 == 0 else (2 if n_pages % 2 == 0 else 1)
    chunk = min(512, nps * 128)
    rows = qblk.shape[1]
    grp = DIFF_GROUP * t_len
    pspec = lambda pi: pl.BlockSpec((None, None, 2 * 128, 128),
                                    lambda b, c, pt, pi=pi: (layer, pt[b, c * nps + pi], 0, 0))
    grid_spec = pltpu.PrefetchScalarGridSpec(
        num_scalar_prefetch=1,
        grid=(bsz, n_pages // nps),
        in_specs=[pl.BlockSpec(lam_vecs.shape, lambda b, c, pt: (0, 0)),
                  pl.BlockSpec((None, rows, 256), lambda b, c, pt: (b, 0, 0)),
                  pl.BlockSpec((None, NEW_PAD, 256), lambda b, c, pt: (b, 0, 0)),
                  pl.BlockSpec((None, NEW_PAD, 256), lambda b, c, pt: (b, 0, 0))]
        + [pspec(pi) for pi in range(nps)] + [pspec(pi) for pi in range(nps)],
        out_specs=pl.BlockSpec((None, DIFF_KV_HEADS, grp, DIFF_V), lambda b, c, pt: (b, 0, 0, 0)),
        scratch_shapes=[pltpu.VMEM((nps * 128, 256), BF16), pltpu.VMEM((nps * 128, 256), BF16),
                        pltpu.VMEM((rows, 1), F32), pltpu.VMEM((rows, 1), F32),
                        pltpu.VMEM((rows, 256), F32)])
    return pl.pallas_call(
        functools.partial(_diff_decode_kernel, n_pages_step=nps, chunk=chunk, t_len=t_len, lam_init=lam_init),
        grid_spec=grid_spec,
        out_shape=jax.ShapeDtypeStruct((bsz, DIFF_KV_HEADS, grp, DIFF_V), F32),
        compiler_params=_cparams("arbitrary", "arbitrary"),
        name="diff_decode",
    )(page_table, lam_vecs, qblk, knew, vnew, *([cache_k] * nps), *([cache_v] * nps))


def _out_proj_kernel(*refs, n_mix, subln_scale):
    mix_refs = refs[:n_mix]
    gate_ref, gsub_ref, w_ref, x_ref, gres_ref, o_ref, z_scr = refs[n_mix:]

    @pl.when(pl.program_id(2) == 0)
    def _():
        gate = _silu(gate_ref[...])
        off = 0
        for mref in mix_refs:
            mix = mref[...]
            wdt = mix.shape[1]
            if subln_scale is None:
                z_scr[:, off:off + wdt] = (mix * gate[:, off:off + wdt]).astype(BF16)
            else:
                for hb in range(wdt // 128):
                    sl = slice(off + hb * 128, off + (hb + 1) * 128)
                    oh = mix[:, hb * 128:(hb + 1) * 128]
                    r = lax.rsqrt(jnp.mean(oh * oh, axis=-1, keepdims=True) + EPS)
                    z_scr[:, sl] = (oh * r * gsub_ref[...] * subln_scale * gate[:, sl]).astype(BF16)
            off += wdt

    o_ref[...] = x_ref[...] + gres_ref[...] * _dot(z_scr[...], w_ref[...])


def out_proj(mixes, proj, gsub, w, x3, gres3, subln_scale):
    gsz, r, d = x3.shape
    tm = min(512, r)
    tn = 1024
    mix_specs = [pl.BlockSpec((None, tm, m.shape[-1]), lambda b, i, j: (b, i, 0)) for m in mixes]
    return pl.pallas_call(
        functools.partial(_out_proj_kernel, n_mix=len(mixes), subln_scale=subln_scale),
        grid=(gsz, r // tm, d // tn),
        in_specs=mix_specs + [pl.BlockSpec((None, tm, d), lambda b, i, j: (b, i, 0)),
                              pl.BlockSpec((1, 128), lambda b, i, j: (0, 0)),
                              pl.BlockSpec((d, tn), lambda b, i, j: (0, j)),
                              pl.BlockSpec((None, tm, tn), lambda b, i, j: (b, i, j)),
                              (pl.BlockSpec((None, 1, tn), lambda b, i, j: (b, 0, j)) if gres3.shape[1] == 1
                               else pl.BlockSpec((None, tm, tn), lambda b, i, j: (b, i, j)))],
        out_specs=pl.BlockSpec((None, tm, tn), lambda b, i, j: (b, i, j)),
        out_shape=jax.ShapeDtypeStruct((gsz, r, d), F32),
        scratch_shapes=[pltpu.VMEM((tm, d), BF16)],
        compiler_params=_cparams("parallel", "parallel", "arbitrary"),
        name="out_proj",
    )(*mixes, proj, gsub, w, x3, gres3)


def _rope_table(pos):
    inv = 1.0 / (ROPE_THETA ** (jnp.arange(0, MLA_ROPE, 2, dtype=F32) / MLA_ROPE))
    ang = pos.astype(F32)[:, None] * inv[None, :]
    cos, sin = jnp.cos(ang), jnp.sin(ang)
    return jnp.concatenate([cos, cos, sin, sin], axis=-1)


def _rot_cols(w):
    half = w.shape[-1] // 2
    return jnp.concatenate([-w[..., half:], w[..., :half]], axis=-1)


def _even_weights(w_in, w_uq, g_q, w_uk, w_uv):
    d = w_in.shape[0]
    u, qd, kvc, pe, gate = (w_in[:, :512], w_in[:, 512:1024], w_in[:, 1024:1536], w_in[:, 1536:1600],
                            w_in[:, 1600:])
    w_ext = jnp.concatenate([gate, u, qd, kvc, pe, _rot_cols(pe), jnp.zeros((d, 128), w_in.dtype)], axis=1)
    q_pe = w_uq[..., MLA_NOPE:]
    wuq = jnp.concatenate([w_uq, _rot_cols(q_pe)], axis=-1).reshape(MLA_RANK, MLA_HEADS * HEAD_BLOCK)
    g_pe = g_q[MLA_NOPE:]
    gq = jnp.concatenate([g_q, g_pe[MLA_ROPE // 2:], g_pe[:MLA_ROPE // 2]]).reshape(1, HEAD_BLOCK)
    return dict(
        w_in=w_ext.astype(BF16), wuq=wuq.astype(BF16), gq=gq,
        wuk=w_uk.reshape(MLA_RANK, MLA_HEADS * MLA_NOPE).astype(BF16),
        wuv=w_uv.reshape(MLA_RANK, MLA_HEADS * MLA_V).astype(BF16),
        wukt=jnp.transpose(w_uk, (1, 2, 0)).astype(BF16),
        wt=jnp.transpose(w_uk, (1, 2, 0)).reshape(MLA_HEADS * MLA_NOPE, MLA_RANK).astype(BF16),
        wuv_h=jnp.transpose(w_uv, (1, 0, 2)).astype(BF16))


def _odd_weights(w_in):
    q, k, v, gate = w_in[:, :2048], w_in[:, 2048:2304], w_in[:, 2304:2560], w_in[:, 2560:]
    return jnp.concatenate([gate, q, k, v], axis=1).astype(BF16)


def kernel(x_prompt, x_sample, cache_mla, cache_diff_k, cache_diff_v, state_pool, page_table, c_prompt, c_sample, norm_g, w_ada, b_ada, w_in_e, w_pool_mix, pool_scale, g_qa, g_kv, w_uq, w_uk, w_uv, g_q, g_k, w_out_e, w_in_o, g_dq, g_dk, lam_q1, lam_k1, lam_q2, lam_k2, g_subln, w_out_o):
    depth = norm_g.shape[0]
    bp, s_len, d = x_prompt.shape
    bs, t_len, _ = x_sample.shape
    n_tok = bs * t_len
    page = cache_mla.shape[2]
    past_len = page_table.shape[1] * page
    assert page == 128 and t_len <= Q_ROWS

    mods = ada_mods(jnp.concatenate([c_prompt, c_sample], axis=0), w_ada, b_ada)
    tab_p = _rope_table(jnp.arange(s_len))
    tab_s = jnp.tile(_rope_table(past_len + jnp.arange(t_len)), (bs, 1))
    cache_k2 = cache_diff_k.reshape(cache_diff_k.shape[0], cache_diff_k.shape[1], page * DIFF_KV_HEADS, 128)
    cache_v2 = cache_diff_v.reshape(cache_diff_v.shape[0], cache_diff_v.shape[1], page * DIFF_KV_HEADS, 128)

    xp = x_prompt
    xs = x_sample.reshape(1, n_tok, d)
    lat_p, lat_s, k_p, k_s, v_p, v_s, tail_p, tail_s = [], [], [], [], [], [], [], []
    for i in range(depth):
        j = i // 2
        shift, scale, gate_res = jnp.split(mods[i], 3, axis=-1)
        mod_p = [m[:bp].reshape(bp, 1, d) for m in (shift, scale, gate_res)]
        mod_s = [jnp.repeat(m[bp:], t_len, axis=0).reshape(1, n_tok, d) for m in (shift, scale, gate_res)]
        if i % 2 == 0:
            ew = _even_weights(w_in_e[j], w_uq[j], g_q[j], w_uk[j], w_uv[j])
            wmix = w_pool_mix[j].astype(BF16)
            psc = pool_scale[j].reshape(1, POOL_WIDTH)
            gqa, gkv, gk = g_qa[j].reshape(1, -1), g_kv[j].reshape(1, -1), g_k[j].reshape(1, -1)
            w_out = w_out_e[j].astype(BF16)
            gsub = jnp.ones((1, 128), F32)

            proj = modnorm_proj(xp, mod_p[1], mod_p[0], norm_g[i], ew["w_in"], tn=1280)
            a_out, lat, q, k, v = even_prep_prompt(proj, tab_p, wmix, psc, gqa, gkv, ew["wuq"], ew["gq"],
                                                   ew["wuk"], gk, ew["wuv"])
            b_out = mla_flash(q, k, v)
            xp = out_proj([a_out, b_out], proj, gsub, w_out, xp, mod_p[2], None)
            lat_p.append(lat)
            tail_p.append(proj[:, s_len - POOL_HIST:, 2048:2048 + POOL_WIDTH])

            proj = modnorm_proj(xs, mod_s[1], mod_s[0], norm_g[i], ew["w_in"], tn=1280)
            u_new = proj[0, :, 2048:2048 + POOL_WIDTH].reshape(bs, t_len, POOL_WIDTH)
            a_out = pool_sample(state_pool[j], u_new, wmix, psc).reshape(1, n_tok, POOL_WIDTH)
            lat, qabs, qpe = even_prep_sample(proj, tab_s, gqa, gkv, ew["wuq"], ew["gq"], gk, ew["wukt"])

            def decode_rows(a, width):
                a = a.reshape(MLA_HEADS, bs, t_len, a.shape[-1])
                a = jnp.pad(a, ((0, 0), (0, 0), (0, Q_ROWS - t_len), (0, width - a.shape[-1])))
                return jnp.transpose(a, (1, 0, 2, 3)).reshape(bs, MLA_HEADS * Q_ROWS, width)

            lat_new = lat.reshape(bs, t_len, MLA_LAT)
            lat_pad = jnp.pad(lat_new, ((0, 0), (0, NEW_PAD - t_len), (0, 0)))
            o_lat = mla_decode(cache_mla, j, page_table, ew["wt"], decode_rows(qabs, MLA_RANK),
                               decode_rows(qpe, 128), lat_pad, t_len)
            o_lat = o_lat.reshape(bs, MLA_HEADS, Q_ROWS, MLA_RANK)[:, :, :t_len]
            o_lat = jnp.transpose(o_lat, (1, 0, 2, 3)).reshape(MLA_HEADS, n_tok, MLA_RANK)
            b_out = value_up(o_lat, ew["wuv_h"]).reshape(1, n_tok, MLA_HEADS * MLA_V)
            xs = out_proj([a_out, b_out], proj, gsub, w_out, xs, mod_s[2], None)
            lat_s.append(lat_new)
            tail_s.append(jnp.concatenate([state_pool[j], u_new], axis=1)[:, -POOL_HIST:])
        else:
            w_in = _odd_weights(w_in_o[j])
            gq128, gk128 = g_dq[j].reshape(1, 128), g_dk[j].reshape(1, 128)
            lam_init = 0.8 - 0.6 * math.exp(-0.3 * i)
            lam_vecs = jnp.stack([lam_q1[j], lam_k1[j], lam_q2[j], lam_k2[j]]).astype(F32)
            w_out = w_out_o[j].astype(BF16)
            gsub = g_subln[j].reshape(1, 128)

            proj = modnorm_proj(xp, mod_p[1], mod_p[0], norm_g[i], w_in, tn=1536)
            qn, kn, kb, vb = odd_prep(proj, gq128, gk128)
            o = diff_flash(lam_vecs, qn, kb, vb, lam_init)
            xp = out_proj([o], proj, gsub, w_out, xp, mod_p[2], 1.0 - lam_init)
            k_p.append(kn.reshape(bp, s_len, DIFF_KV_HEADS, 128))
            v_p.append(proj[:, :, 4352:4608].reshape(bp, s_len, DIFF_KV_HEADS, 128))

            proj = modnorm_proj(xs, mod_s[1], mod_s[0], norm_g[i], w_in, tn=1536)
            qn, kn, kb, vb = odd_prep(proj, gq128, gk128)
            q6 = qn.reshape(bs, t_len, DIFF_KV_HEADS, DIFF_GROUP, 2, DIFF_DIM)
            q6 = jnp.transpose(q6, (0, 2, 4, 3, 1, 5))
            slot = jnp.eye(2 * DIFF_KV_HEADS, dtype=BF16).reshape(DIFF_KV_HEADS, 2, 2 * DIFF_KV_HEADS)
            qblk = (q6[..., None, :] * slot[None, :, :, None, None, :, None]).reshape(
                bs, 2 * DIFF_KV_HEADS * DIFF_GROUP * t_len, 2 * DIFF_KV_HEADS * DIFF_DIM)
            pad_new = lambda a: jnp.pad(a.reshape(bs, t_len, 256), ((0, 0), (0, NEW_PAD - t_len), (0, 0)))
            o = diff_decode(cache_k2, cache_v2, j, page_table, lam_vecs, qblk, pad_new(kb), pad_new(vb),
                            t_len, lam_init)
            o = o.reshape(bs, DIFF_KV_HEADS, DIFF_GROUP, t_len, DIFF_V)
            o = jnp.transpose(o, (0, 3, 1, 2, 4)).reshape(1, n_tok, DIFF_HEADS * DIFF_V)
            xs = out_proj([o], proj, gsub, w_out, xs, mod_s[2], 1.0 - lam_init)
            k_s.append(kn.reshape(bs, t_len, DIFF_KV_HEADS, 128))
            v_s.append(proj[0, :, 4352:4608].reshape(bs, t_len, DIFF_KV_HEADS, 128))

    return (xp, xs.reshape(bs, t_len, d), jnp.stack(lat_p), jnp.stack(lat_s), jnp.stack(k_p), jnp.stack(k_s),
            jnp.stack(v_p), jnp.stack(v_s), jnp.stack(tail_p), jnp.stack(tail_s))
```

```python
import functools
import math

import jax
import jax.numpy as jnp
from jax import lax
from jax.experimental import pallas as pl
from jax.experimental.pallas import tpu as pltpu

F32 = jnp.float32
BF16 = jnp.bfloat16

EPS = 1e-6
NEG_INF = -1e30
ROPE_THETA = 10000.0

POOL_WINDOWS = (2, 4, 8, 16)
POOL_GROUP = 128
POOL_WIDTH = 512
POOL_HIST = 15
HIST_ROWS = 16

MLA_NOPE = 128
MLA_ROPE = 64
MLA_QK = 192
MLA_V = 128
MLA_HEADS = 12
MLA_RANK = 512
MLA_LAT = 576
HEAD_BLOCK = 256
Q_ROWS = 8
PROJECT_AHEAD = 1

DIFF_DIM = 64
DIFF_HEADS = 16
DIFF_KV_HEADS = 2
DIFF_GROUP = 8
DIFF_V = 128

PAGE = 128
NEW_PAD = 128

LOG2E = math.log2(math.e)
MLA_SCALE = MLA_QK ** -0.5 * LOG2E
DIFF_SCALE = DIFF_DIM ** -0.5 * LOG2E

VMEM_LIMIT = 56 * 1024 * 1024


def _cparams(*sem):
    return pltpu.CompilerParams(dimension_semantics=sem, vmem_limit_bytes=VMEM_LIMIT)


def _dot(a, b):
    return jnp.dot(a, b, preferred_element_type=F32)


def _dot_nt(a, b):
    return lax.dot_general(a, b, (((1,), (1,)), ((), ())), preferred_element_type=F32)


def _silu(x):
    return x * jax.nn.sigmoid(x)


def _lane_lo(shape):
    return (lax.broadcasted_iota(jnp.int32, shape, len(shape) - 1) % 128) < 64


def _pages_per_step(n_pages, want):
    nps = want
    while n_pages % nps:
        nps //= 2
    return nps


def _ada_kernel(c_ref, w_ref, b_ref, o_ref):
    s = _silu(c_ref[...]).astype(BF16)
    o_ref[...] = _dot(s, w_ref[...].astype(BF16)) + b_ref[...]


def ada_mods(c_all, w_ada, b_ada):
    depth, d, n3 = w_ada.shape
    mc = c_all.shape[0]
    tn = 512
    return pl.pallas_call(
        _ada_kernel,
        grid=(depth, n3 // tn),
        in_specs=[pl.BlockSpec((mc, d), lambda i, j: (0, 0)),
                  pl.BlockSpec((None, d, tn), lambda i, j: (i, 0, j)),
                  pl.BlockSpec((None, 1, tn), lambda i, j: (i, 0, j))],
        out_specs=pl.BlockSpec((None, mc, tn), lambda i, j: (i, 0, j)),
        out_shape=jax.ShapeDtypeStruct((depth, mc, n3), F32),
        compiler_params=_cparams("parallel", "parallel"),
        name="ada_mods",
    )(c_all, w_ada, b_ada.reshape(depth, 1, n3))


def _modnorm_proj_kernel(x_ref, sc_ref, sh_ref, g_ref, w_ref, o_ref, h_scr):
    @pl.when(pl.program_id(2) == 0)
    def _():
        x = x_ref[...]
        xn = x * lax.rsqrt(jnp.mean(x * x, axis=-1, keepdims=True) + EPS)
        h = xn * g_ref[...] * (1.0 + sc_ref[...]) + sh_ref[...]
        h_scr[...] = h.astype(BF16)

    o_ref[...] = _dot(h_scr[...], w_ref[...])


def _mod_spec(mod, tm):
    d = mod.shape[-1]
    if mod.shape[1] == 1:
        return pl.BlockSpec((None, 1, d), lambda b, i, j: (b, 0, 0))
    return pl.BlockSpec((None, tm, d), lambda b, i, j: (b, i, 0))


def modnorm_proj(x3, scale3, shift3, g, w, tn):
    gsz, r, d = x3.shape
    n = w.shape[1]
    tm = min(512, r)
    return pl.pallas_call(
        _modnorm_proj_kernel,
        grid=(gsz, r // tm, n // tn),
        in_specs=[pl.BlockSpec((None, tm, d), lambda b, i, j: (b, i, 0)),
                  _mod_spec(scale3, tm), _mod_spec(shift3, tm),
                  pl.BlockSpec((1, d), lambda b, i, j: (0, 0)),
                  pl.BlockSpec((d, tn), lambda b, i, j: (0, j))],
        out_specs=pl.BlockSpec((None, tm, tn), lambda b, i, j: (b, i, j)),
        out_shape=jax.ShapeDtypeStruct((gsz, r, n), F32),
        scratch_shapes=[pltpu.VMEM((tm, d), BF16)],
        compiler_params=_cparams("parallel", "parallel", "arbitrary"),
        name="modnorm_proj",
    )(x3, scale3, shift3, g.reshape(1, d), w)


def _latent(kvc, pe_blk, tab, gkv):
    c = kvc * lax.rsqrt(jnp.mean(kvc * kvc, axis=-1, keepdims=True) + EPS) * gkv
    u = pe_blk * tab
    kpe = (u + pltpu.roll(u, 64, 1))[:, :MLA_ROPE]
    return c, kpe


def _query_heads(qd, gqa, wuq, gq, tab, scale):
    qn = (qd * lax.rsqrt(jnp.mean(qd * qd, axis=-1, keepdims=True) + EPS) * gqa).astype(BF16)
    q_all = _dot(qn, wuq)
    g_nope = gq[:, :MLA_NOPE]
    rope_mul = tab * gq[:, MLA_NOPE:]
    lo = _lane_lo((qd.shape[0], 128))
    heads = []
    for h in range(MLA_HEADS):
        nope = q_all[:, h * HEAD_BLOCK:h * HEAD_BLOCK + MLA_NOPE]
        t = q_all[:, h * HEAD_BLOCK + MLA_NOPE:(h + 1) * HEAD_BLOCK]
        ss = (jnp.sum(nope * nope, axis=-1, keepdims=True)
              + jnp.sum(jnp.where(lo, t * t, 0.0), axis=-1, keepdims=True))
        r = lax.rsqrt(ss * (1.0 / MLA_QK) + EPS) * scale
        u = t * rope_mul
        pe = (u + pltpu.roll(u, 64, 1))[:, :MLA_ROPE] * r
        heads.append((nope * g_nope * r, pe))
    return heads


def _even_prep_prompt_kernel(u_ref, up_ref, qd_ref, kvc_ref, pe_ref, tab_ref, wmix_ref, psc_ref,
                             gqa_ref, gkv_ref, wuq_ref, gq_ref, wuk_ref, gk_ref, wuv_ref,
                             a_ref, lat_ref, q_ref, k_ref, v_ref, uext_scr, *, tm):
    i = pl.program_id(1)
    tab = tab_ref[...]

    u = u_ref[...]
    hist = up_ref[...]
    uext_scr[0:HIST_ROWS, :] = jnp.where(i > 0, hist, jnp.zeros_like(hist))
    uext_scr[HIST_ROWS:, :] = u
    pos = i * tm + lax.broadcasted_iota(jnp.int32, (tm, 1), 0)
    for gi, w in enumerate(POOL_WINDOWS):
        sl = slice(gi * POOL_GROUP, (gi + 1) * POOL_GROUP)
        acc = u[:, sl]
        for k in range(1, w):
            acc = acc + uext_scr[HIST_ROWS - k:HIST_ROWS - k + tm, sl]
        cnt = jnp.minimum(w, pos + 1).astype(F32)
        a = (acc / cnt - u[:, sl]).astype(BF16)
        a_ref[:, sl] = _dot(a, wmix_ref[gi]) * psc_ref[:, sl]

    c, kpe = _latent(kvc_ref[...], pe_ref[...], tab, gkv_ref[...])
    lat_ref[:, :MLA_RANK] = c
    lat_ref[:, MLA_RANK:] = kpe
    heads = _query_heads(qd_ref[...], gqa_ref[...], wuq_ref[...], gq_ref[...], tab, MLA_SCALE)
    cb = c.astype(BF16)
    k_all = _dot(cb, wuk_ref[...])
    v_all = _dot(cb, wuv_ref[...])
    pe_ss = jnp.sum(kpe * kpe, axis=-1, keepdims=True)
    gk = gk_ref[...]
    for h in range(MLA_HEADS):
        nope, pe = heads[h]
        q_ref[h, :, :MLA_NOPE] = nope.astype(BF16)
        q_ref[h, :, MLA_NOPE:] = pe.astype(BF16)
        kn = k_all[:, h * MLA_NOPE:(h + 1) * MLA_NOPE]
        r = lax.rsqrt((jnp.sum(kn * kn, axis=-1, keepdims=True) + pe_ss) * (1.0 / MLA_QK) + EPS)
        k_ref[h, :, :MLA_NOPE] = (kn * gk * r).astype(BF16)
        k_ref[h, :, MLA_NOPE:] = (kpe * r).astype(BF16)
        v_ref[h] = v_all[:, h * MLA_V:(h + 1) * MLA_V].astype(BF16)


def even_prep_prompt(proj, tab, wmix, psc, gqa, gkv, wuq, gq, wuk, gk, wuv):
    b, s, _ = proj.shape
    tm = min(256, s)
    hb = tm // HIST_ROWS
    full = lambda a: pl.BlockSpec(a.shape, lambda bb, i: (0,) * a.ndim)
    head_out = lambda w: pl.BlockSpec((None, MLA_HEADS, tm, w), lambda bb, i: (bb, 0, i, 0))
    return pl.pallas_call(
        functools.partial(_even_prep_prompt_kernel, tm=tm),
        grid=(b, s // tm),
        in_specs=[pl.BlockSpec((None, tm, 512), lambda bb, i: (bb, i, 4)),
                  pl.BlockSpec((None, HIST_ROWS, 512), lambda bb, i: (bb, jnp.maximum(i * hb - 1, 0), 4)),
                  pl.BlockSpec((None, tm, 512), lambda bb, i: (bb, i, 5)),
                  pl.BlockSpec((None, tm, 512), lambda bb, i: (bb, i, 6)),
                  pl.BlockSpec((None, tm, 128), lambda bb, i: (bb, i, 28)),
                  pl.BlockSpec((tm, 128), lambda bb, i: (i, 0)),
                  full(wmix), full(psc), full(gqa), full(gkv), full(wuq), full(gq), full(wuk), full(gk),
                  full(wuv)],
        out_specs=[pl.BlockSpec((None, tm, POOL_WIDTH), lambda bb, i: (bb, i, 0)),
                   pl.BlockSpec((None, tm, MLA_LAT), lambda bb, i: (bb, i, 0)),
                   head_out(MLA_QK), head_out(MLA_QK), head_out(MLA_V)],
        out_shape=[jax.ShapeDtypeStruct((b, s, POOL_WIDTH), F32),
                   jax.ShapeDtypeStruct((b, s, MLA_LAT), F32),
                   jax.ShapeDtypeStruct((b, MLA_HEADS, s, MLA_QK), BF16),
                   jax.ShapeDtypeStruct((b, MLA_HEADS, s, MLA_QK), BF16),
                   jax.ShapeDtypeStruct((b, MLA_HEADS, s, MLA_V), BF16)],
        scratch_shapes=[pltpu.VMEM((HIST_ROWS + tm, POOL_WIDTH), F32)],
        compiler_params=_cparams("parallel", "parallel"),
        name="even_prep_prompt",
    )(proj, proj, proj, proj, proj, tab, wmix, psc, gqa, gkv, wuq, gq, wuk, gk, wuv)


def _even_prep_sample_kernel(qd_ref, kvc_ref, pe_ref, tab_ref, gqa_ref, gkv_ref, wuq_ref, gq_ref,
                             gk_ref, wukt_ref, lat_ref, qabs_ref, qpe_ref):
    tab = tab_ref[...]
    c, kpe = _latent(kvc_ref[...], pe_ref[...], tab, gkv_ref[...])
    lat_ref[:, :MLA_RANK] = c
    lat_ref[:, MLA_RANK:] = kpe
    heads = _query_heads(qd_ref[...], gqa_ref[...], wuq_ref[...], gq_ref[...], tab, MLA_SCALE)
    gk = gk_ref[...]
    for h in range(MLA_HEADS):
        nope, pe = heads[h]
        qabs_ref[h] = _dot((nope * gk).astype(BF16), wukt_ref[h]).astype(BF16)
        qpe_ref[h] = pe.astype(BF16)


def even_prep_sample(proj, tab, gqa, gkv, wuq, gq, gk, wukt):
    _, t, _ = proj.shape
    tm = min(256, t)
    full = lambda a: pl.BlockSpec(a.shape, lambda i: (0,) * a.ndim)
    return pl.pallas_call(
        _even_prep_sample_kernel,
        grid=(t // tm,),
        in_specs=[pl.BlockSpec((None, tm, 512), lambda i: (0, i, 5)),
                  pl.BlockSpec((None, tm, 512), lambda i: (0, i, 6)),
                  pl.BlockSpec((None, tm, 128), lambda i: (0, i, 28)),
                  pl.BlockSpec((tm, 128), lambda i: (i, 0)),
                  full(gqa), full(gkv), full(wuq), full(gq), full(gk), full(wukt)],
        out_specs=[pl.BlockSpec((tm, MLA_LAT), lambda i: (i, 0)),
                   pl.BlockSpec((MLA_HEADS, tm, MLA_RANK), lambda i: (0, i, 0)),
                   pl.BlockSpec((MLA_HEADS, tm, MLA_ROPE), lambda i: (0, i, 0))],
        out_shape=[jax.ShapeDtypeStruct((t, MLA_LAT), F32),
                   jax.ShapeDtypeStruct((MLA_HEADS, t, MLA_RANK), BF16),
                   jax.ShapeDtypeStruct((MLA_HEADS, t, MLA_ROPE), BF16)],
        compiler_params=_cparams("parallel"),
        name="even_prep_sample",
    )(proj, proj, proj, tab, gqa, gkv, wuq, gq, gk, wukt)


def _pool_sample_kernel(st_ref, u_ref, wmix_ref, psc_ref, a_ref, *, t_len):
    rows = [st_ref[:, r, :] for r in range(POOL_HIST)] + [u_ref[:, r, :] for r in range(t_len)]
    for t in range(t_len):
        cur = rows[POOL_HIST + t]
        for gi, w in enumerate(POOL_WINDOWS):
            sl = slice(gi * POOL_GROUP, (gi + 1) * POOL_GROUP)
            acc = cur[:, sl]
            for k in range(1, w):
                acc = acc + rows[POOL_HIST + t - k][:, sl]
            a = (acc / float(w) - cur[:, sl]).astype(BF16)
            a_ref[:, t, sl] = _dot(a, wmix_ref[gi]) * psc_ref[:, sl]


def pool_sample(state, u, wmix, psc):
    b, t_len, _ = u.shape
    tb = min(64, b)
    full = lambda a: pl.BlockSpec(a.shape, lambda i: (0,) * a.ndim)
    return pl.pallas_call(
        functools.partial(_pool_sample_kernel, t_len=t_len),
        grid=(b // tb,),
        in_specs=[pl.BlockSpec((tb, POOL_HIST, POOL_WIDTH), lambda i: (i, 0, 0)),
                  pl.BlockSpec((tb, t_len, POOL_WIDTH), lambda i: (i, 0, 0)),
                  full(wmix), full(psc)],
        out_specs=pl.BlockSpec((tb, t_len, POOL_WIDTH), lambda i: (i, 0, 0)),
        out_shape=jax.ShapeDtypeStruct((b, t_len, POOL_WIDTH), F32),
        compiler_params=_cparams("parallel"),
        name="pool_sample",
    )(state, u, wmix, psc)


def _mla_flash_kernel(q_ref, k_ref, v_ref, o_ref, m_scr, l_scr, acc_scr, *, tq, hp):
    i = pl.program_id(2)
    m_scr[...] = jnp.full_like(m_scr, NEG_INF)
    l_scr[...] = jnp.zeros_like(l_scr)
    acc_scr[...] = jnp.zeros_like(acc_scr)

    def tile(j, masked):
        off = pl.multiple_of(j * tq, tq)
        scores = [_dot_nt(q_ref[h], k_ref[h, pl.ds(off, tq), :]) for h in range(hp)]
        for h in range(hp):
            s = scores[h]
            if masked:
                row = lax.broadcasted_iota(jnp.int32, s.shape, 0)
                col = lax.broadcasted_iota(jnp.int32, s.shape, 1)
                s = jnp.where(col <= row, s, NEG_INF)
            m_prev = m_scr[h]
            m_new = jnp.maximum(m_prev, jnp.max(s, axis=-1, keepdims=True))
            alpha = jnp.exp2(m_prev - m_new)
            p = jnp.exp2(s - m_new)
            l_scr[h] = alpha * l_scr[h] + jnp.sum(p, axis=-1, keepdims=True)
            acc_scr[h] = alpha * acc_scr[h] + _dot(p.astype(BF16), v_ref[h, pl.ds(off, tq), :])
            m_scr[h] = m_new

    def body(j, carry):
        tile(j, False)
        return carry

    lax.fori_loop(0, i, body, 0)
    tile(i, True)
    for h in range(hp):
        o_ref[:, h * MLA_V:(h + 1) * MLA_V] = acc_scr[h] / l_scr[h]


def mla_flash(q, k, v):
    b, h, s, _ = q.shape
    tq = min(512, s)
    hp = 4
    return pl.pallas_call(
        functools.partial(_mla_flash_kernel, tq=tq, hp=hp),
        grid=(b, h // hp, s // tq),
        in_specs=[pl.BlockSpec((None, hp, tq, MLA_QK), lambda bb, hh, i: (bb, hh, i, 0)),
                  pl.BlockSpec((None, hp, s, MLA_QK), lambda bb, hh, i: (bb, hh, 0, 0)),
                  pl.BlockSpec((None, hp, s, MLA_V), lambda bb, hh, i: (bb, hh, 0, 0))],
        out_specs=pl.BlockSpec((None, tq, hp * MLA_V), lambda bb, hh, i: (bb, i, hh)),
        out_shape=jax.ShapeDtypeStruct((b, s, h * MLA_V), F32),
        scratch_shapes=[pltpu.VMEM((hp, tq, 1), F32), pltpu.VMEM((hp, tq, 1), F32),
                        pltpu.VMEM((hp, tq, MLA_V), F32)],
        compiler_params=_cparams("parallel", "parallel", "arbitrary"),
        name="mla_flash",
    )(q, k, v)


def _mla_decode_kernel(pt_ref, wt_ref, qabs_ref, qpe_ref, new_ref, *rest, n_pages_step, chunk_pages, t_len):
    pages = rest[:n_pages_step]
    o_ref = rest[n_pages_step]
    lhs_scr, m_scr, l_scr, acc_scr = rest[n_pages_step + 1:]
    b = pl.program_id(0)
    c = pl.program_id(1)
    nk = MLA_HEADS * MLA_NOPE
    nq = MLA_HEADS * Q_ROWS

    @pl.when(jnp.logical_and(b == 0, c == 0))
    def _():
        lhs_scr[0:nk, :] = wt_ref[...]

    @pl.when(c == 0)
    def _():
        lhs_scr[nk:, :] = qabs_ref[...]
        m_scr[...] = jnp.full_like(m_scr, NEG_INF)
        l_scr[...] = jnp.zeros_like(l_scr)
        acc_scr[...] = jnp.zeros_like(acc_scr)

    def project(page_refs):
        lat_t = jnp.concatenate([r[0:MLA_RANK, :].astype(BF16) for r in page_refs], axis=1)
        return lat_t, _dot(lhs_scr[...], lat_t)

    def attend(page_refs, lat_t, big, m_prev, l_prev, acc_prev, masked):
        pe_t = jnp.concatenate([r[MLA_RANK:, :] for r in page_refs], axis=1)
        s_pe = _dot(qpe_ref[...], pe_t.astype(BF16))
        pe_ss = jnp.sum(pe_t * pe_t, axis=0, keepdims=True)
        parts = []
        for h in range(MLA_HEADS):
            kh = big[h * MLA_NOPE:(h + 1) * MLA_NOPE, :]
            ss = jnp.sum(kh * kh, axis=0, keepdims=True) + pe_ss
            r = lax.rsqrt(ss * (1.0 / MLA_QK) + EPS)
            sl = slice(h * Q_ROWS, (h + 1) * Q_ROWS)
            parts.append((big[nk + h * Q_ROWS:nk + (h + 1) * Q_ROWS, :] + s_pe[sl, :]) * r)
        s = jnp.concatenate(parts, axis=0)
        if masked:
            t = lax.broadcasted_iota(jnp.int32, s.shape, 0) % Q_ROWS
            col = lax.broadcasted_iota(jnp.int32, s.shape, 1)
            s = jnp.where(col <= jnp.minimum(t, t_len - 1), s, NEG_INF)
        m_new = jnp.maximum(m_prev, jnp.max(s, axis=-1, keepdims=True))
        alpha = jnp.exp2(m_prev - m_new)
        p = jnp.exp2(s - m_new)
        l_new = alpha * l_prev + jnp.sum(p, axis=-1, keepdims=True)
        acc_new = alpha * acc_prev + _dot_nt(p.astype(BF16), lat_t)
        return m_new, l_new, acc_new

    chunks = [pages[ci * chunk_pages:(ci + 1) * chunk_pages] for ci in range(n_pages_step // chunk_pages)]
    state = (m_scr[...], l_scr[...], acc_scr[...])
    ahead = [project(ch) for ch in chunks[:PROJECT_AHEAD]]
    for ci, chunk in enumerate(chunks):
        if ci + PROJECT_AHEAD < len(chunks):
            ahead.append(project(chunks[ci + PROJECT_AHEAD]))
        state = attend(chunk, *ahead.pop(0), *state, False)
    m_scr[...], l_scr[...], acc_scr[...] = state

    @pl.when(c == pl.num_programs(1) - 1)
    def _():
        _, l_fin, acc_fin = attend([new_ref], *project([new_ref]), m_scr[...], l_scr[...], acc_scr[...], True)
        o_ref[...] = acc_fin / l_fin


def mla_decode(cache_t, layer, page_table, wt, qabs, qpe, lat_new_t, t_len):
    bsz, n_pages = page_table.shape
    nps = _pages_per_step(n_pages, 32)
    chunk_pages = 8 if nps % 8 == 0 else nps
    nk = MLA_HEADS * MLA_NOPE
    nq = MLA_HEADS * Q_ROWS
    page_specs = [pl.BlockSpec((None, None, MLA_LAT, PAGE),
                               lambda b, c, pt, pi=pi: (layer, pt[b, c * nps + pi], 0, 0))
                  for pi in range(nps)]
    grid_spec = pltpu.PrefetchScalarGridSpec(
        num_scalar_prefetch=1,
        grid=(bsz, n_pages // nps),
        in_specs=[pl.BlockSpec((nk, MLA_RANK), lambda b, c, pt: (0, 0)),
                  pl.BlockSpec((None, nq, MLA_RANK), lambda b, c, pt: (b, 0, 0)),
                  pl.BlockSpec((None, nq, MLA_ROPE), lambda b, c, pt: (b, 0, 0)),
                  pl.BlockSpec((None, MLA_LAT, NEW_PAD), lambda b, c, pt: (b, 0, 0))] + page_specs,
        out_specs=pl.BlockSpec((None, nq, MLA_RANK), lambda b, c, pt: (b, 0, 0)),
        scratch_shapes=[pltpu.VMEM((nk + nq, MLA_RANK), BF16),
                        pltpu.VMEM((nq, 1), F32), pltpu.VMEM((nq, 1), F32),
                        pltpu.VMEM((nq, MLA_RANK), F32)])
    return pl.pallas_call(
        functools.partial(_mla_decode_kernel, n_pages_step=nps, chunk_pages=chunk_pages, t_len=t_len),
        grid_spec=grid_spec,
        out_shape=jax.ShapeDtypeStruct((bsz, nq, MLA_RANK), F32),
        compiler_params=_cparams("arbitrary", "arbitrary"),
        name="mla_decode",
    )(page_table, wt, qabs, qpe, lat_new_t, *([cache_t] * nps))


def _value_up_kernel(o_ref, w_ref, out_ref):
    out_ref[...] = _dot(o_ref[...].astype(BF16), w_ref[...])


def value_up(o_lat, wuv):
    h, t, r = o_lat.shape
    return pl.pallas_call(
        _value_up_kernel,
        grid=(h,),
        in_specs=[pl.BlockSpec((None, t, r), lambda i: (i, 0, 0)),
                  pl.BlockSpec((None, r, MLA_V), lambda i: (i, 0, 0))],
        out_specs=pl.BlockSpec((t, MLA_V), lambda i: (0, i)),
        out_shape=jax.ShapeDtypeStruct((t, h * MLA_V), F32),
        compiler_params=_cparams("parallel"),
        name="value_up",
    )(o_lat, wuv)


def _seg_norm(x, g128, scale):
    lo = _lane_lo((x.shape[0], 128))
    out = []
    for hb in range(x.shape[1] // 128):
        xh = x[:, hb * 128:(hb + 1) * 128]
        sq = xh * xh
        s0 = jnp.sum(jnp.where(lo, sq, 0.0), axis=-1, keepdims=True)
        s1 = jnp.sum(jnp.where(lo, 0.0, sq), axis=-1, keepdims=True)
        r = jnp.where(lo, lax.rsqrt(s0 * (1.0 / DIFF_DIM) + EPS), lax.rsqrt(s1 * (1.0 / DIFF_DIM) + EPS))
        out.append(xh * r * (g128 * scale))
    return out


def _odd_prep_kernel(q_ref, k_ref, v_ref, gq_ref, gk_ref, qo_ref, ko_ref, kb_ref, vb_ref, *, transposed):
    for hb, blk in enumerate(_seg_norm(q_ref[...], gq_ref[...], DIFF_SCALE)):
        if transposed:
            qo_ref[hb] = blk.T.astype(BF16)
        else:
            qo_ref[:, hb * 128:(hb + 1) * 128] = blk.astype(BF16)
    for hb, blk in enumerate(_seg_norm(k_ref[...], gk_ref[...], 1.0)):
        ko_ref[:, hb * 128:(hb + 1) * 128] = blk
        kb_ref[:, hb * 128:(hb + 1) * 128] = blk.astype(BF16)
    if transposed:
        vb_ref[...] = v_ref[...].T.astype(BF16)
    else:
        vb_ref[...] = v_ref[...].astype(BF16)


def odd_prep(proj, gq128, gk128, transposed, tm):
    gsz, r, _ = proj.shape
    tm = min(tm, r)
    qw, kw = DIFF_HEADS * 128, DIFF_KV_HEADS * 128
    full = lambda a: pl.BlockSpec(a.shape, lambda b, i: (0,) * a.ndim)
    row = lambda w: pl.BlockSpec((None, tm, w), lambda b, i: (b, i, 0))
    if transposed:
        q_spec = pl.BlockSpec((None, DIFF_HEADS, 128, tm), lambda b, i: (b, 0, 0, i))
        q_shape = jax.ShapeDtypeStruct((gsz, DIFF_HEADS, 128, r), BF16)
        v_spec = pl.BlockSpec((None, None, kw, tm), lambda b, i: (b, i, 0, 0))
        v_shape = jax.ShapeDtypeStruct((gsz, r // tm, kw, tm), BF16)
    else:
        q_spec, q_shape = row(qw), jax.ShapeDtypeStruct((gsz, r, qw), BF16)
        v_spec, v_shape = row(kw), jax.ShapeDtypeStruct((gsz, r, kw), BF16)
    return pl.pallas_call(
        functools.partial(_odd_prep_kernel, transposed=transposed),
        grid=(gsz, r // tm),
        in_specs=[pl.BlockSpec((None, tm, qw), lambda b, i: (b, i, 1)),
                  pl.BlockSpec((None, tm, kw), lambda b, i: (b, i, 16)),
                  pl.BlockSpec((None, tm, kw), lambda b, i: (b, i, 17)),
                  full(gq128), full(gk128)],
        out_specs=[q_spec, row(kw), row(kw), v_spec],
        out_shape=[q_shape,
                   jax.ShapeDtypeStruct((gsz, r, kw), F32),
                   jax.ShapeDtypeStruct((gsz, r, kw), BF16),
                   v_shape],
        compiler_params=_cparams("parallel", "parallel"),
        name="odd_prep",
    )(proj, proj, proj, gq128, gk128)


def _lambda(lam_ref, lam_init):
    v = lam_ref[...]
    a = jnp.sum(v[0:1] * v[1:2], axis=-1, keepdims=True)
    c = jnp.sum(v[2:3] * v[3:4], axis=-1, keepdims=True)
    return jnp.exp(a) - jnp.exp(c) + lam_init


def _diff_flash_kernel(lam_ref, qt_ref, k_ref, vt_ref, o_ref, qs_scr, m_scr, l_scr, acc_scr, *, tq, tk, lam_init):
    i = pl.program_id(1)
    half = DIFF_GROUP * tq
    row_lo = lax.broadcasted_iota(jnp.int32, (128, tq), 0) < DIFF_DIM
    for kh in range(DIFF_KV_HEADS):
        for g in range(DIFF_GROUP):
            blk = qt_ref[kh * DIFF_GROUP + g]
            zero = jnp.zeros_like(blk)
            qs_scr[kh, :, g * tq:(g + 1) * tq] = jnp.where(row_lo, blk, zero)
            qs_scr[kh, :, half + g * tq:half + (g + 1) * tq] = jnp.where(row_lo, zero, blk)
    m_scr[...] = jnp.full_like(m_scr, NEG_INF)
    l_scr[...] = jnp.zeros_like(l_scr)
    acc_scr[...] = jnp.zeros_like(acc_scr)

    def tile(j, masked):
        off = pl.multiple_of(j * tk, tk)
        scores = [_dot(k_ref[pl.ds(off, tk), kh * 128:(kh + 1) * 128], qs_scr[kh])
                  for kh in range(DIFF_KV_HEADS)]
        for kh in range(DIFF_KV_HEADS):
            ksl = slice(kh * 128, (kh + 1) * 128)
            s = scores[kh]
            if masked:
                kpos = j * tk + lax.broadcasted_iota(jnp.int32, s.shape, 0)
                qpos = i * tq + lax.broadcasted_iota(jnp.int32, s.shape, 1) % tq
                s = jnp.where(kpos <= qpos, s, NEG_INF)
            m_prev = m_scr[kh]
            m_new = jnp.maximum(m_prev, jnp.max(s, axis=0, keepdims=True))
            alpha = jnp.exp2(m_prev - m_new)
            p = jnp.exp2(s - m_new)
            l_scr[kh] = alpha * l_scr[kh] + jnp.sum(p, axis=0, keepdims=True)
            acc_scr[kh] = alpha * acc_scr[kh] + _dot(vt_ref[j, ksl, :], p.astype(BF16))
            m_scr[kh] = m_new

    def body(j, carry):
        tile(j, False)
        return carry

    n_full = (i * tq) // tk
    lax.fori_loop(0, n_full, body, 0)
    tile(n_full, True)

    lam = _lambda(lam_ref, lam_init)
    for kh in range(DIFF_KV_HEADS):
        o_all = acc_scr[kh] / l_scr[kh]
        o = o_all[:, :half] - lam * o_all[:, half:]
        for g in range(DIFF_GROUP):
            hh = kh * DIFF_GROUP + g
            o_ref[:, hh * 128:(hh + 1) * 128] = o[:, g * tq:(g + 1) * tq].T


def diff_flash(lam_vecs, qt, k, vt, lam_init, tq, tk):
    b, _, _, s = qt.shape
    n = 2 * DIFF_GROUP * tq
    qw = DIFF_HEADS * DIFF_V
    assert tk % tq == 0 and s % tk == 0
    return pl.pallas_call(
        functools.partial(_diff_flash_kernel, tq=tq, tk=tk, lam_init=lam_init),
        grid=(b, s // tq),
        in_specs=[pl.BlockSpec(lam_vecs.shape, lambda bb, i: (0, 0)),
                  pl.BlockSpec((None, DIFF_HEADS, 128, tq), lambda bb, i: (bb, 0, 0, i)),
                  pl.BlockSpec((None, s, k.shape[-1]), lambda bb, i: (bb, 0, 0)),
                  pl.BlockSpec((None,) + vt.shape[1:], lambda bb, i: (bb, 0, 0, 0))],
        out_specs=pl.BlockSpec((None, tq, qw), lambda bb, i: (bb, i, 0)),
        out_shape=jax.ShapeDtypeStruct((b, s, qw), F32),
        scratch_shapes=[pltpu.VMEM((DIFF_KV_HEADS, 128, n), BF16),
                        pltpu.VMEM((DIFF_KV_HEADS, 1, n), F32),
                        pltpu.VMEM((DIFF_KV_HEADS, 1, n), F32),
                        pltpu.VMEM((DIFF_KV_HEADS, DIFF_V, n), F32)],
        compiler_params=_cparams("parallel", "arbitrary"),
        name="diff_flash",
    )(lam_vecs, qt, k, vt)


def _diff_decode_kernel(pt_ref, lam_ref, q_ref, knew_ref, vnew_ref, *rest, n_pages_step, t_len, lam_init):
    kpages = rest[:n_pages_step]
    vpages = rest[n_pages_step:2 * n_pages_step]
    o_ref = rest[2 * n_pages_step]
    k_scr, v_scr, m_scr, l_scr, acc_scr = rest[2 * n_pages_step + 1:]
    c = pl.program_id(1)
    grp = DIFF_GROUP * t_len

    @pl.when(c == 0)
    def _():
        m_scr[...] = jnp.full_like(m_scr, NEG_INF)
        l_scr[...] = jnp.zeros_like(l_scr)
        acc_scr[...] = jnp.zeros_like(acc_scr)

    def attend(kc, vc, masked):
        s = _dot_nt(q_ref[...], kc)
        if masked:
            t = lax.broadcasted_iota(jnp.int32, s.shape, 0) % t_len
            col = lax.broadcasted_iota(jnp.int32, s.shape, 1)
            s = jnp.where(col <= t, s, NEG_INF)
        m_prev = m_scr[...]
        m_new = jnp.maximum(m_prev, jnp.max(s, axis=-1, keepdims=True))
        alpha = jnp.exp2(m_prev - m_new)
        p = jnp.exp2(s - m_new)
        l_scr[...] = alpha * l_scr[...] + jnp.sum(p, axis=-1, keepdims=True)
        acc_scr[...] = alpha * acc_scr[...] + _dot(p.astype(BF16), vc)
        m_scr[...] = m_new

    for pi in range(n_pages_step):
        for kh in range(DIFF_KV_HEADS):
            k_scr[pi * PAGE:(pi + 1) * PAGE, kh * 128:(kh + 1) * 128] = (
                kpages[pi][pl.ds(kh, PAGE, stride=DIFF_KV_HEADS), :].astype(BF16))
            v_scr[pi * PAGE:(pi + 1) * PAGE, kh * 128:(kh + 1) * 128] = (
                vpages[pi][pl.ds(kh, PAGE, stride=DIFF_KV_HEADS), :].astype(BF16))
    attend(k_scr[...], v_scr[...], False)

    @pl.when(c == pl.num_programs(1) - 1)
    def _():
        attend(knew_ref[...], vnew_ref[...], True)
        o_all = acc_scr[...] / l_scr[...]
        lam = _lambda(lam_ref, lam_init)
        for kh in range(DIFF_KV_HEADS):
            base = kh * 2 * grp
            vsl = slice(kh * DIFF_V, (kh + 1) * DIFF_V)
            o_ref[kh] = o_all[base:base + grp, vsl] - lam * o_all[base + grp:base + 2 * grp, vsl]


def diff_decode(cache_k, cache_v, layer, page_table, lam_vecs, qblk, knew, vnew, t_len, lam_init):
    bsz, n_pages = page_table.shape
    nps = _pages_per_step(n_pages, 16)
    rows = qblk.shape[1]
    grp = DIFF_GROUP * t_len
    pspec = lambda pi: pl.BlockSpec((None, None, DIFF_KV_HEADS * PAGE, 128),
                                    lambda b, c, pt, pi=pi: (layer, pt[b, c * nps + pi], 0, 0))
    grid_spec = pltpu.PrefetchScalarGridSpec(
        num_scalar_prefetch=1,
        grid=(bsz, n_pages // nps),
        in_specs=[pl.BlockSpec(lam_vecs.shape, lambda b, c, pt: (0, 0)),
                  pl.BlockSpec((None, rows, 256), lambda b, c, pt: (b, 0, 0)),
                  pl.BlockSpec((None, NEW_PAD, 256), lambda b, c, pt: (b, 0, 0)),
                  pl.BlockSpec((None, NEW_PAD, 256), lambda b, c, pt: (b, 0, 0))]
        + [pspec(pi) for pi in range(nps)] + [pspec(pi) for pi in range(nps)],
        out_specs=pl.BlockSpec((None, DIFF_KV_HEADS, grp, DIFF_V), lambda b, c, pt: (b, 0, 0, 0)),
        scratch_shapes=[pltpu.VMEM((nps * PAGE, 256), BF16), pltpu.VMEM((nps * PAGE, 256), BF16),
                        pltpu.VMEM((rows, 1), F32), pltpu.VMEM((rows, 1), F32),
                        pltpu.VMEM((rows, 256), F32)])
    return pl.pallas_call(
        functools.partial(_diff_decode_kernel, n_pages_step=nps, t_len=t_len, lam_init=lam_init),
        grid_spec=grid_spec,
        out_shape=jax.ShapeDtypeStruct((bsz, DIFF_KV_HEADS, grp, DIFF_V), F32),
        compiler_params=_cparams("arbitrary", "arbitrary"),
        name="diff_decode",
    )(page_table, lam_vecs, qblk, knew, vnew, *([cache_k] * nps), *([cache_v] * nps))


def _out_proj_kernel(*refs, n_mix, subln_scale):
    mix_refs = refs[:n_mix]
    gate_ref, gsub_ref, w_ref, x_ref, gres_ref, o_ref, z_scr = refs[n_mix:]

    @pl.when(pl.program_id(2) == 0)
    def _():
        gate = _silu(gate_ref[...])
        off = 0
        for mref in mix_refs:
            mix = mref[...]
            wdt = mix.shape[1]
            if subln_scale is None:
                z_scr[:, off:off + wdt] = (mix * gate[:, off:off + wdt]).astype(BF16)
            else:
                for hb in range(wdt // 128):
                    sl = slice(off + hb * 128, off + (hb + 1) * 128)
                    oh = mix[:, hb * 128:(hb + 1) * 128]
                    r = lax.rsqrt(jnp.mean(oh * oh, axis=-1, keepdims=True) + EPS)
                    z_scr[:, sl] = (oh * r * gsub_ref[...] * subln_scale * gate[:, sl]).astype(BF16)
            off += wdt

    o_ref[...] = x_ref[...] + gres_ref[...] * _dot(z_scr[...], w_ref[...])


def out_proj(mixes, proj, gsub, w, x3, gres3, subln_scale):
    gsz, r, d = x3.shape
    tm = min(512, r)
    tn = 1024
    mix_specs = [pl.BlockSpec((None, tm, m.shape[-1]), lambda b, i, j: (b, i, 0)) for m in mixes]
    return pl.pallas_call(
        functools.partial(_out_proj_kernel, n_mix=len(mixes), subln_scale=subln_scale),
        grid=(gsz, r // tm, d // tn),
        in_specs=mix_specs + [pl.BlockSpec((None, tm, d), lambda b, i, j: (b, i, 0)),
                              pl.BlockSpec((1, 128), lambda b, i, j: (0, 0)),
                              pl.BlockSpec((d, tn), lambda b, i, j: (0, j)),
                              pl.BlockSpec((None, tm, tn), lambda b, i, j: (b, i, j)),
                              (pl.BlockSpec((None, 1, tn), lambda b, i, j: (b, 0, j)) if gres3.shape[1] == 1
                               else pl.BlockSpec((None, tm, tn), lambda b, i, j: (b, i, j)))],
        out_specs=pl.BlockSpec((None, tm, tn), lambda b, i, j: (b, i, j)),
        out_shape=jax.ShapeDtypeStruct((gsz, r, d), F32),
        scratch_shapes=[pltpu.VMEM((tm, d), BF16)],
        compiler_params=_cparams("parallel", "parallel", "arbitrary"),
        name="out_proj",
    )(*mixes, proj, gsub, w, x3, gres3)


def _rope_table(pos):
    inv = 1.0 / (ROPE_THETA ** (jnp.arange(0, MLA_ROPE, 2, dtype=F32) / MLA_ROPE))
    ang = pos.astype(F32)[:, None] * inv[None, :]
    cos, sin = jnp.cos(ang), jnp.sin(ang)
    return jnp.concatenate([cos, cos, sin, sin], axis=-1)


def _rot_cols(w):
    half = w.shape[-1] // 2
    return jnp.concatenate([-w[..., half:], w[..., :half]], axis=-1)


def _even_weights(w_in, w_uq, g_q, w_uk, w_uv):
    d = w_in.shape[0]
    u, qd, kvc, pe, gate = (w_in[:, :512], w_in[:, 512:1024], w_in[:, 1024:1536], w_in[:, 1536:1600],
                            w_in[:, 1600:])
    w_ext = jnp.concatenate([gate, u, qd, kvc, pe, _rot_cols(pe), jnp.zeros((d, 128), w_in.dtype)], axis=1)
    q_pe = w_uq[..., MLA_NOPE:]
    wuq = jnp.concatenate([w_uq, _rot_cols(q_pe)], axis=-1).reshape(MLA_RANK, MLA_HEADS * HEAD_BLOCK)
    g_pe = g_q[MLA_NOPE:]
    gq = jnp.concatenate([g_q, g_pe[MLA_ROPE // 2:], g_pe[:MLA_ROPE // 2]]).reshape(1, HEAD_BLOCK)
    return dict(
        w_in=w_ext.astype(BF16), wuq=wuq.astype(BF16), gq=gq,
        wuk=w_uk.reshape(MLA_RANK, MLA_HEADS * MLA_NOPE).astype(BF16),
        wuv=w_uv.reshape(MLA_RANK, MLA_HEADS * MLA_V).astype(BF16),
        wukt=jnp.transpose(w_uk, (1, 2, 0)).astype(BF16),
        wt=jnp.transpose(w_uk, (1, 2, 0)).reshape(MLA_HEADS * MLA_NOPE, MLA_RANK).astype(BF16),
        wuv_h=jnp.transpose(w_uv, (1, 0, 2)).astype(BF16))


def _odd_weights(w_in):
    q, k, v, gate = w_in[:, :2048], w_in[:, 2048:2304], w_in[:, 2304:2560], w_in[:, 2560:]
    return jnp.concatenate([gate, q, k, v], axis=1).astype(BF16)


def kernel(x_prompt, x_sample, cache_mla, cache_diff_k, cache_diff_v, state_pool, page_table, c_prompt, c_sample, norm_g, w_ada, b_ada, w_in_e, w_pool_mix, pool_scale, g_qa, g_kv, w_uq, w_uk, w_uv, g_q, g_k, w_out_e, w_in_o, g_dq, g_dk, lam_q1, lam_k1, lam_q2, lam_k2, g_subln, w_out_o):
    depth = norm_g.shape[0]
    bp, s_len, d = x_prompt.shape
    bs, t_len, _ = x_sample.shape
    n_tok = bs * t_len
    past_len = page_table.shape[1] * PAGE
    assert cache_mla.shape[2] == PAGE and t_len <= Q_ROWS

    mods = ada_mods(jnp.concatenate([c_prompt, c_sample], axis=0), w_ada, b_ada)
    tab_p = _rope_table(jnp.arange(s_len))
    tab_s = jnp.tile(_rope_table(past_len + jnp.arange(t_len)), (bs, 1))
    cache_t = jnp.swapaxes(cache_mla, 2, 3)
    cache_k2 = cache_diff_k.reshape(cache_diff_k.shape[0], cache_diff_k.shape[1], PAGE * DIFF_KV_HEADS, 128)
    cache_v2 = cache_diff_v.reshape(cache_diff_v.shape[0], cache_diff_v.shape[1], PAGE * DIFF_KV_HEADS, 128)
    diff_tq, diff_tk = min(128, s_len), min(256, s_len)

    xp = x_prompt
    xs = x_sample.reshape(1, n_tok, d)
    lat_p, lat_s, k_p, k_s, v_p, v_s, tail_p, tail_s = [], [], [], [], [], [], [], []
    for i in range(depth):
        j = i // 2
        shift, scale, gate_res = jnp.split(mods[i], 3, axis=-1)
        mod_p = [m[:bp].reshape(bp, 1, d) for m in (shift, scale, gate_res)]
        mod_s = [jnp.repeat(m[bp:], t_len, axis=0).reshape(1, n_tok, d) for m in (shift, scale, gate_res)]
        if i % 2 == 0:
            ew = _even_weights(w_in_e[j], w_uq[j], g_q[j], w_uk[j], w_uv[j])
            wmix = w_pool_mix[j].astype(BF16)
            psc = pool_scale[j].reshape(1, POOL_WIDTH)
            gqa, gkv, gk = g_qa[j].reshape(1, -1), g_kv[j].reshape(1, -1), g_k[j].reshape(1, -1)
            w_out = w_out_e[j].astype(BF16)
            gsub = jnp.ones((1, 128), F32)

            proj = modnorm_proj(xp, mod_p[1], mod_p[0], norm_g[i], ew["w_in"], tn=1280)
            a_out, lat, q, k, v = even_prep_prompt(proj, tab_p, wmix, psc, gqa, gkv, ew["wuq"], ew["gq"],
                                                   ew["wuk"], gk, ew["wuv"])
            b_out = mla_flash(q, k, v)
            xp = out_proj([a_out, b_out], proj, gsub, w_out, xp, mod_p[2], None)
            lat_p.append(lat)
            tail_p.append(proj[:, s_len - POOL_HIST:, 2048:2048 + POOL_WIDTH])

            proj = modnorm_proj(xs, mod_s[1], mod_s[0], norm_g[i], ew["w_in"], tn=1280)
            u_new = proj[0, :, 2048:2048 + POOL_WIDTH].reshape(bs, t_len, POOL_WIDTH)
            a_out = pool_sample(state_pool[j], u_new, wmix, psc).reshape(1, n_tok, POOL_WIDTH)
            lat, qabs, qpe = even_prep_sample(proj, tab_s, gqa, gkv, ew["wuq"], ew["gq"], gk, ew["wukt"])

            def decode_rows(a):
                a = a.reshape(MLA_HEADS, bs, t_len, a.shape[-1])
                a = jnp.pad(a, ((0, 0), (0, 0), (0, Q_ROWS - t_len), (0, 0)))
                return jnp.transpose(a, (1, 0, 2, 3)).reshape(bs, MLA_HEADS * Q_ROWS, a.shape[-1])

            lat_new = lat.reshape(bs, t_len, MLA_LAT)
            lat_new_t = jnp.pad(jnp.swapaxes(lat_new, 1, 2), ((0, 0), (0, 0), (0, NEW_PAD - t_len)))
            o_lat = mla_decode(cache_t, j, page_table, ew["wt"], decode_rows(qabs), decode_rows(qpe),
                               lat_new_t, t_len)
            o_lat = o_lat.reshape(bs, MLA_HEADS, Q_ROWS, MLA_RANK)[:, :, :t_len]
            o_lat = jnp.transpose(o_lat, (1, 0, 2, 3)).reshape(MLA_HEADS, n_tok, MLA_RANK)
            b_out = value_up(o_lat, ew["wuv_h"]).reshape(1, n_tok, MLA_HEADS * MLA_V)
            xs = out_proj([a_out, b_out], proj, gsub, w_out, xs, mod_s[2], None)
            lat_s.append(lat_new)
            tail_s.append(jnp.concatenate([state_pool[j], u_new], axis=1)[:, -POOL_HIST:])
        else:
            w_in = _odd_weights(w_in_o[j])
            gq128, gk128 = g_dq[j].reshape(1, 128), g_dk[j].reshape(1, 128)
            lam_init = 0.8 - 0.6 * math.exp(-0.3 * i)
            lam_vecs = jnp.stack([lam_q1[j], lam_k1[j], lam_q2[j], lam_k2[j]]).astype(F32)
            w_out = w_out_o[j].astype(BF16)
            gsub = g_subln[j].reshape(1, 128)

            proj = modnorm_proj(xp, mod_p[1], mod_p[0], norm_g[i], w_in, tn=1536)
            qt, kn, kb, vt = odd_prep(proj, gq128, gk128, True, diff_tk)
            o = diff_flash(lam_vecs, qt, kb, vt, lam_init, diff_tq, diff_tk)
            xp = out_proj([o], proj, gsub, w_out, xp, mod_p[2], 1.0 - lam_init)
            k_p.append(kn.reshape(bp, s_len, DIFF_KV_HEADS, 128))
            v_p.append(proj[:, :, 4352:4608].reshape(bp, s_len, DIFF_KV_HEADS, 128))

            proj = modnorm_proj(xs, mod_s[1], mod_s[0], norm_g[i], w_in, tn=1536)
            qn, kn, kb, vb = odd_prep(proj, gq128, gk128, False, 512)
            q6 = qn.reshape(bs, t_len, DIFF_KV_HEADS, DIFF_GROUP, 2, DIFF_DIM)
            q6 = jnp.transpose(q6, (0, 2, 4, 3, 1, 5))
            slot = jnp.eye(2 * DIFF_KV_HEADS, dtype=BF16).reshape(DIFF_KV_HEADS, 2, 2 * DIFF_KV_HEADS)
            qblk = (q6[..., None, :] * slot[None, :, :, None, None, :, None]).reshape(
                bs, 2 * DIFF_KV_HEADS * DIFF_GROUP * t_len, 2 * DIFF_KV_HEADS * DIFF_DIM)
            pad_new = lambda a: jnp.pad(a.reshape(bs, t_len, 256), ((0, 0), (0, NEW_PAD - t_len), (0, 0)))
            o = diff_decode(cache_k2, cache_v2, j, page_table, lam_vecs, qblk, pad_new(kb), pad_new(vb),
                            t_len, lam_init)
            o = o.reshape(bs, DIFF_KV_HEADS, DIFF_GROUP, t_len, DIFF_V)
            o = jnp.transpose(o, (0, 3, 1, 2, 4)).reshape(1, n_tok, DIFF_HEADS * DIFF_V)
            xs = out_proj([o], proj, gsub, w_out, xs, mod_s[2], 1.0 - lam_init)
            k_s.append(kn.reshape(bs, t_len, DIFF_KV_HEADS, 128))
            v_s.append(proj[0, :, 4352:4608].reshape(bs, t_len, DIFF_KV_HEADS, 128))

    return (xp, xs.reshape(bs, t_len, d), jnp.stack(lat_p), jnp.stack(lat_s), jnp.stack(k_p), jnp.stack(k_s),
            jnp.stack(v_p), jnp.stack(v_s), jnp.stack(tail_p), jnp.stack(tail_s))
```
